```python
import jax, jax.numpy as jnp
from jax import lax
import numpy as np

D_MODEL = 1024
BATCH = 8
SEQ = 2048
DEPTH = 4
DEC_BATCH = 128
DEC_SEQ = 4
PAST_LEN = 16384
PAGE_SIZE = 128

D_CONV = D_MODEL // 2
CONV_W = 31
HEAD_SIZE = 64
D_RWKV = D_MODEL
N_HEADS_RWKV = D_RWKV // HEAD_SIZE
LORA_DECAY = 64
LORA_AAA = 64
LORA_GATE = 160
D_FF = 3 * D_MODEL
FFN_CONV_W = 3
N_BRANCH = 2
D_RW_IN = 3 * D_RWKV + LORA_DECAY + LORA_AAA + LORA_GATE
D_IN = 2 * D_CONV + D_RW_IN + N_BRANCH * D_MODEL
RMS_EPS = 1e-6
LN_EPS = 1e-5
GN_EPS = 64e-5

kernel_name = "hybrid_conformer_rwkv7_convffn_step"


def rmsnorm(x, g):
    xf = x.astype(jnp.float32)
    y = xf * lax.rsqrt(jnp.mean(xf * xf, axis=-1, keepdims=True) + RMS_EPS)
    return (y * g.astype(jnp.float32)).astype(x.dtype)


def layernorm(x, g, b):
    xf = x.astype(jnp.float32)
    mu = jnp.mean(xf, axis=-1, keepdims=True)
    var = jnp.mean(jnp.square(xf - mu), axis=-1, keepdims=True)
    return (xf - mu) * lax.rsqrt(var + LN_EPS) * g.astype(jnp.float32) + b.astype(jnp.float32)


def causal_dwconv(buf, u, w, b):
    width = w.shape[0]
    full = jnp.concatenate([buf.astype(u.dtype), u], axis=1)
    y = lax.conv_general_dilated(full, w.astype(u.dtype)[:, None, :], window_strides=(1,),
                                 padding='VALID', dimension_numbers=('NWC', 'WIO', 'NWC'),
                                 feature_group_count=u.shape[-1])
    return y + b.astype(u.dtype), full[:, full.shape[1] - (width - 1):]


def wkv7_scan(S0, r, w, k, a, b, v):
    def step(S, inp):
        r_t, w_t, k_t, a_t, b_t, v_t = inp
        sa = jnp.einsum('bhij,bhj->bhi', S, a_t)
        S = S * w_t[:, :, None, :] + sa[..., :, None] * b_t[..., None, :] + v_t[..., :, None] * k_t[..., None, :]
        y = jnp.einsum('bhij,bhj->bhi', S, r_t)
        return S, y
    xs = tuple(jnp.moveaxis(t, 1, 0) for t in (r, w, k, a, b, v))
    S, y = lax.scan(step, S0, xs)
    return jnp.moveaxis(y, 0, 1), S


def layer(x, conv_buf, shift_buf, wkv_state, ffn_buf, p):
    (norm_mix_g, w_in, conv_dw_w, conv_dw_b, conv_ln_g, conv_ln_b, w_conv_out,
     rw_mu, rw_w0, rw_w2, rw_a0, rw_a2, rw_g2, rw_k_k, rw_k_a, rw_r_k, rw_ln_g, rw_ln_b, w_rw_out,
     w_mix_out, norm_ffn_g, w_up, ffn_dw_w, ffn_dw_b, w_down) = p
    f32 = jnp.float32
    B, T, _ = x.shape
    H, N = N_HEADS_RWKV, HEAD_SIZE

    h = rmsnorm(x, norm_mix_g)
    z = h @ w_in
    zc = z[..., :2 * D_CONV]
    zr = z[..., 2 * D_CONV:2 * D_CONV + D_RW_IN]
    zg = z[..., 2 * D_CONV + D_RW_IN:]

    u = zc[..., :D_CONV] * jax.nn.sigmoid(zc[..., D_CONV:])
    c, new_conv = causal_dwconv(conv_buf, u, conv_dw_w, conv_dw_b)
    c = jax.nn.silu(layernorm(c, conv_ln_g, conv_ln_b))
    ya = c.astype(x.dtype) @ w_conv_out

    zr32 = zr.astype(f32)
    prev = jnp.concatenate([shift_buf.astype(f32)[:, None, :], zr32[:, :-1]], axis=1)
    new_shift = zr[:, -1]
    xs = zr32 + (prev - zr32) * rw_mu.astype(f32)
    o = 0
    r = xs[..., o:o + D_RWKV]; o += D_RWKV
    k = xs[..., o:o + D_RWKV]; o += D_RWKV
    v = xs[..., o:o + D_RWKV]; o += D_RWKV
    wl = xs[..., o:o + LORA_DECAY]; o += LORA_DECAY
    al = xs[..., o:o + LORA_AAA]; o += LORA_AAA
    gl = xs[..., o:o + LORA_GATE]
    wlog = -jax.nn.softplus(-(rw_w0.astype(f32) + jnp.tanh(wl) @ rw_w2.astype(f32))) - 0.5
    decay = jnp.exp(-jnp.exp(wlog))
    alr = jax.nn.sigmoid(rw_a0.astype(f32) + al @ rw_a2.astype(f32))
    gout = jax.nn.sigmoid(gl) @ rw_g2.astype(f32)
    kk = (k * rw_k_k.astype(f32)).reshape(B, T, H, N)
    kk = kk / jnp.maximum(jnp.sqrt(jnp.sum(kk * kk, axis=-1, keepdims=True)), 1e-12)
    k = k * (1.0 + (alr - 1.0) * rw_k_a.astype(f32))
    rh = r.reshape(B, T, H, N); kh = k.reshape(B, T, H, N); vh = v.reshape(B, T, H, N)
    ah = alr.reshape(B, T, H, N)
    yh, new_wkv = wkv7_scan(wkv_state.astype(f32), rh, decay.reshape(B, T, H, N), kh, -kk, kk * ah, vh)
    mu = jnp.mean(yh, axis=-1, keepdims=True)
    var = jnp.mean(jnp.square(yh - mu), axis=-1, keepdims=True)
    yh = (yh - mu) * lax.rsqrt(var + GN_EPS)
    y = yh.reshape(B, T, D_RWKV) * rw_ln_g.astype(f32) + rw_ln_b.astype(f32)
    bonus = jnp.sum(rh * kh * rw_r_k.astype(f32), axis=-1, keepdims=True) * vh
    y = (y + bonus.reshape(B, T, D_RWKV)) * gout
    yb = y.astype(x.dtype) @ w_rw_out

    gates = jax.nn.sigmoid(zg.astype(f32))
    m = gates[..., :D_MODEL] * ya.astype(f32) + gates[..., D_MODEL:] * yb.astype(f32)
    x = x + (m.astype(x.dtype) @ w_mix_out)

    h2 = rmsnorm(x, norm_ffn_g)
    up = h2 @ w_up
    cu, new_ffn = causal_dwconv(ffn_buf, up, ffn_dw_w, ffn_dw_b)
    f = jax.nn.gelu(cu[..., :D_FF], approximate=True) * cu[..., D_FF:]
    x = x + f @ w_down
    return x, new_conv, new_shift, new_wkv, new_ffn


def trunk(x, conv0, shift0, wkv0, ffn0, params, norm_final_g):
    convs, shifts, wkvs, ffns = [], [], [], []
    for l in range(DEPTH):
        p = tuple(t[l] for t in params)
        x, c, s, w, f = layer(x, conv0[l], shift0[l], wkv0[l], ffn0[l], p)
        convs.append(c); shifts.append(s); wkvs.append(w); ffns.append(f)
    y = rmsnorm(x, norm_final_g)
    return y, jnp.stack(convs), jnp.stack(shifts), jnp.stack(wkvs), jnp.stack(ffns)


def setup_inputs(seed: int = 0) -> dict:
    key = jax.random.key(seed)
    ks = iter(jax.random.split(key, 40))

    def nrm(shape, scale):
        return jax.random.normal(next(ks), shape, jnp.float32) * scale

    def unif(shape, lo, hi):
        return jax.random.uniform(next(ks), shape, jnp.float32, lo, hi)

    L, D = DEPTH, D_MODEL
    H, N = N_HEADS_RWKV, HEAD_SIZE
    return {
        "x_prompt": nrm((BATCH, SEQ, D), 1.0),
        "x_sample": nrm((DEC_BATCH, DEC_SEQ, D), 1.0),
        "state_conv": nrm((L, DEC_BATCH, CONV_W - 1, D_CONV), 0.5),
        "state_shift": nrm((L, DEC_BATCH, D_RW_IN), 1.0),
        "state_wkv": nrm((L, DEC_BATCH, H, N, N), 0.3),
        "state_ffn": nrm((L, DEC_BATCH, FFN_CONV_W - 1, 2 * D_FF), 1.0),
        "norm_mix_g": 1.0 + nrm((L, D), 0.02),
        "w_in": nrm((L, D, D_IN), D ** -0.5),
        "conv_dw_w": nrm((L, CONV_W, D_CONV), CONV_W ** -0.5),
        "conv_dw_b": nrm((L, D_CONV), 0.02),
        "conv_ln_g": 1.0 + nrm((L, D_CONV), 0.02),
        "conv_ln_b": nrm((L, D_CONV), 0.02),
        "w_conv_out": nrm((L, D_CONV, D), D_CONV ** -0.5),
        "rw_mu": unif((L, D_RW_IN), 0.0, 1.0),
        "rw_w0": unif((L, D_RWKV), -5.0, 0.5),
        "rw_w2": nrm((L, LORA_DECAY, D_RWKV), 0.5 * LORA_DECAY ** -0.5),
        "rw_a0": nrm((L, D_RWKV), 0.1),
        "rw_a2": nrm((L, LORA_AAA, D_RWKV), 0.5 * LORA_AAA ** -0.5),
        "rw_g2": nrm((L, LORA_GATE, D_RWKV), LORA_GATE ** -0.5),
        "rw_k_k": 0.85 + nrm((L, D_RWKV), 0.05),
        "rw_k_a": 1.0 + nrm((L, D_RWKV), 0.05),
        "rw_r_k": nrm((L, H, N), 0.1),
        "rw_ln_g": 1.0 + nrm((L, D_RWKV), 0.02),
        "rw_ln_b": nrm((L, D_RWKV), 0.02),
        "w_rw_out": nrm((L, D_RWKV, D), D_RWKV ** -0.5),
        "w_mix_out": nrm((L, D, D), D ** -0.5),
        "norm_ffn_g": 1.0 + nrm((L, D), 0.02),
        "w_up": nrm((L, D, 2 * D_FF), D ** -0.5),
        "ffn_dw_w": nrm((L, FFN_CONV_W, 2 * D_FF), FFN_CONV_W ** -0.5),
        "ffn_dw_b": nrm((L, 2 * D_FF), 0.02),
        "w_down": nrm((L, D_FF, D), D_FF ** -0.5),
        "norm_final_g": 1.0 + nrm((D,), 0.02),
    }


def reference(x_prompt, x_sample, state_conv, state_shift, state_wkv, state_ffn,
              norm_mix_g, w_in, conv_dw_w, conv_dw_b, conv_ln_g, conv_ln_b, w_conv_out,
              rw_mu, rw_w0, rw_w2, rw_a0, rw_a2, rw_g2, rw_k_k, rw_k_a, rw_r_k, rw_ln_g, rw_ln_b, w_rw_out,
              w_mix_out, norm_ffn_g, w_up, ffn_dw_w, ffn_dw_b, w_down, norm_final_g):
    params = (norm_mix_g, w_in, conv_dw_w, conv_dw_b, conv_ln_g, conv_ln_b, w_conv_out,
              rw_mu, rw_w0, rw_w2, rw_a0, rw_a2, rw_g2, rw_k_k, rw_k_a, rw_r_k, rw_ln_g, rw_ln_b, w_rw_out,
              w_mix_out, norm_ffn_g, w_up, ffn_dw_w, ffn_dw_b, w_down)
    B = x_prompt.shape[0]
    dt = x_prompt.dtype
    conv0 = jnp.zeros((DEPTH, B, CONV_W - 1, D_CONV), dt)
    shift0 = jnp.zeros((DEPTH, B, D_RW_IN), dt)
    wkv0 = jnp.zeros((DEPTH, B, N_HEADS_RWKV, HEAD_SIZE, HEAD_SIZE), jnp.float32)
    ffn0 = jnp.zeros((DEPTH, B, FFN_CONV_W - 1, 2 * D_FF), dt)
    y_prompt, p_conv, p_shift, p_wkv, p_ffn = trunk(x_prompt, conv0, shift0, wkv0, ffn0, params, norm_final_g)
    y_sample, s_conv, s_shift, s_wkv, s_ffn = trunk(x_sample, state_conv, state_shift, state_wkv, state_ffn,
                                                    params, norm_final_g)
    return (y_prompt, y_sample, p_conv, p_shift, p_wkv, p_ffn, s_conv, s_shift, s_wkv, s_ffn)
```

```python
import functools

import jax
import jax.numpy as jnp
from jax import lax
from jax.experimental import pallas as pl
from jax.experimental.pallas import tpu as pltpu

F32 = jnp.float32
BF16 = jnp.bfloat16

D_MODEL = 1024
D_CONV = D_MODEL // 2
CONV_W = 31
HEAD = 64
N_HEADS = D_MODEL // HEAD
LORA_DECAY = 64
LORA_AAA = 64
LORA_GATE = 160
D_FF = 3 * D_MODEL
FFN_CONV_W = 3
D_RKV = 3 * D_MODEL
D_RW_IN = D_RKV + LORA_DECAY + LORA_AAA + LORA_GATE
RMS_EPS = 1e-6
LN_EPS = 1e-5
GN_EPS = 64e-5

SUBLANES_V7X = 8
LANES_V7X = 128
VMEM_LIMIT_BYTES_V7X = 56 * 1024 * 1024

LORA_PAD = 4 * LANES_V7X
LORA_W_OFF, LORA_A_OFF, LORA_G_OFF = 0, LANES_V7X, 2 * LANES_V7X

ROW_TILE = 128
PROMPT_CHUNK = 64
SAMPLE_CHUNK = 8


def _cparams(n_grid):
    return pltpu.CompilerParams(dimension_semantics=("arbitrary",) * n_grid,
                                vmem_limit_bytes=VMEM_LIMIT_BYTES_V7X)


def _const_spec(shape):
    return pl.BlockSpec(shape, lambda *_: (0,) * len(shape), pipeline_mode=pl.Buffered(1))


def _dot(a, b):
    return jnp.dot(a, b, preferred_element_type=F32)


def _rmsnorm(x, g):
    return x * lax.rsqrt(jnp.mean(x * x, axis=-1, keepdims=True) + RMS_EPS) * g


def _inproj_kernel(x_ref, g_ref, wc_ref, wrkv_ref, wlora_ref, wg_ref,
                   u_ref, zrkv_ref, zlora_ref, gate_ref):
    hb = _rmsnorm(x_ref[...], g_ref[...]).astype(BF16)
    zc = _dot(hb, wc_ref[...])
    u_ref[...] = zc[:, :D_CONV] * jax.nn.sigmoid(zc[:, D_CONV:])
    zrkv_ref[...] = _dot(hb, wrkv_ref[...])
    zlora_ref[...] = _dot(hb, wlora_ref[...])
    gate_ref[...] = jax.nn.sigmoid(_dot(hb, wg_ref[...]))


def _inproj(x, g, wc, wrkv, wlora, wg):
    rows = x.shape[0]
    n = ROW_TILE
    row = lambda c: pl.BlockSpec((n, c), lambda i: (i, 0))
    return pl.pallas_call(
        _inproj_kernel,
        grid=(rows // n,),
        in_specs=[row(D_MODEL), _const_spec(g.shape), _const_spec(wc.shape), _const_spec(wrkv.shape),
                  _const_spec(wlora.shape), _const_spec(wg.shape)],
        out_specs=[row(D_CONV), row(D_RKV), row(LORA_PAD), row(2 * D_MODEL)],
        out_shape=[jax.ShapeDtypeStruct((rows, D_CONV), F32), jax.ShapeDtypeStruct((rows, D_RKV), F32),
                   jax.ShapeDtypeStruct((rows, LORA_PAD), F32), jax.ShapeDtypeStruct((rows, 2 * D_MODEL), F32)],
        compiler_params=_cparams(1),
        name="inproj",
    )(x, g, wc, wrkv, wlora, wg)


CONV_ROW_BLOCK = 64


def _conv_kernel(*refs, n, n_steps, stride, has_hist):
    if has_hist:
        u_ref, hist_ref, w_ref, b_ref, lg_ref, lb_ref, c_ref, newhist_ref, full_scr = refs
    else:
        u_ref, w_ref, b_ref, lg_ref, lb_ref, c_ref, newhist_ref, full_scr = refs
    hist_rows = (CONV_W - 1) * stride
    pad = (-hist_rows) % SUBLANES_V7X
    base = pad + hist_rows

    @pl.when(pl.program_id(1) == 0)
    def _():
        if has_hist:
            if pad:
                full_scr[0:pad, :] = jnp.zeros((pad, D_CONV), F32)
            full_scr[pad:base, :] = hist_ref[...]
        else:
            full_scr[0:base, :] = jnp.zeros((base, D_CONV), F32)

    full_scr[base:base + n, :] = u_ref[...]
    w = w_ref[...]
    for rb in range(n // CONV_ROW_BLOCK):
        r0 = rb * CONV_ROW_BLOCK
        acc = jnp.broadcast_to(b_ref[...], (CONV_ROW_BLOCK, D_CONV))
        for j in range(CONV_W):
            s = pad + r0 + j * stride
            acc = acc + w[j:j + 1, :] * full_scr[s:s + CONV_ROW_BLOCK, :]
        mu = jnp.mean(acc, axis=-1, keepdims=True)
        xc = acc - mu
        var = jnp.mean(xc * xc, axis=-1, keepdims=True)
        y = xc * lax.rsqrt(var + LN_EPS) * lg_ref[...] + lb_ref[...]
        c_ref[r0:r0 + CONV_ROW_BLOCK, :] = (y * jax.nn.sigmoid(y)).astype(BF16)

    newhist_ref[...] = full_scr[pad + n:base + n, :]
    if n_steps > 1:
        full_scr[0:base, :] = full_scr[n:n + base, :]


def _conv_branch(u, hist, w, b, lg, lb, *, n, stride):
    nb, t, _ = u.shape
    has_hist = hist is not None
    hist_rows = (CONV_W - 1) * stride
    base = hist_rows + (-hist_rows) % SUBLANES_V7X
    tile = pl.BlockSpec((None, n, D_CONV), lambda i, j: (i, j, 0))
    hspec = pl.BlockSpec((None, hist_rows, D_CONV), lambda i, j: (i, 0, 0))
    ins = [u] + ([hist] if has_hist else []) + [w, b, lg, lb]
    in_specs = [tile] + ([hspec] if has_hist else []) + [_const_spec(a.shape) for a in (w, b, lg, lb)]
    return pl.pallas_call(
        functools.partial(_conv_kernel, n=n, n_steps=t // n, stride=stride, has_hist=has_hist),
        grid=(nb, t // n),
        in_specs=in_specs,
        out_specs=[tile, hspec],
        out_shape=[jax.ShapeDtypeStruct((nb, t, D_CONV), BF16), jax.ShapeDtypeStruct((nb, hist_rows, D_CONV), F32)],
        scratch_shapes=[pltpu.VMEM((base + n, D_CONV), F32)],
        compiler_params=_cparams(2),
        name="conv_branch",
    )(*ins)


def _prep_kernel(*refs, n, stride, has_state):
    if has_state:
        (zrkv_ref, zlora_ref, srkv_ref, slora_ref, murkv_ref, mulora_ref, w0_ref, w2_ref, a0_ref, a2_ref, g2_ref,
         r_ref, k_ref, v_ref, lw_ref, alr_ref, go_ref, frkv_scr, flora_scr) = refs
    else:
        (zrkv_ref, zlora_ref, murkv_ref, mulora_ref, w0_ref, w2_ref, a0_ref, a2_ref, g2_ref,
         r_ref, k_ref, v_ref, lw_ref, alr_ref, go_ref, frkv_scr, flora_scr) = refs
    p = max(SUBLANES_V7X, stride)

    @pl.when(pl.program_id(1) == 0)
    def _():
        if has_state:
            frkv_scr[0:p, :] = srkv_ref[...]
            flora_scr[0:p, :] = slora_ref[...]
        else:
            frkv_scr[0:p, :] = jnp.zeros((p, D_RKV), F32)
            flora_scr[0:p, :] = jnp.zeros((p, LORA_PAD), F32)

    zrkv = zrkv_ref[...]
    zlora = zlora_ref[...]
    frkv_scr[p:p + n, :] = zrkv
    flora_scr[p:p + n, :] = zlora
    xs = zrkv + (frkv_scr[p - stride:p - stride + n, :] - zrkv) * murkv_ref[...]
    xl = zlora + (flora_scr[p - stride:p - stride + n, :] - zlora) * mulora_ref[...]
    frkv_scr[0:p, :] = frkv_scr[n:n + p, :]
    flora_scr[0:p, :] = flora_scr[n:n + p, :]

    r_ref[...] = xs[:, 0:D_MODEL]
    k_ref[...] = xs[:, D_MODEL:2 * D_MODEL]
    v_ref[...] = xs[:, 2 * D_MODEL:3 * D_MODEL]

    wl = jnp.tanh(xl[:, LORA_W_OFF:LORA_W_OFF + LANES_V7X]).astype(BF16)
    al = xl[:, LORA_A_OFF:LORA_A_OFF + LANES_V7X].astype(BF16)
    gl = jax.nn.sigmoid(xl[:, LORA_G_OFF:LORA_PAD]).astype(BF16)
    t = -(w0_ref[...] + _dot(wl, w2_ref[...]))
    softplus = jnp.maximum(t, 0.0) + jnp.log(1.0 + jnp.exp(-jnp.abs(t)))
    lw_ref[...] = -jnp.exp(-softplus - 0.5)
    alr_ref[...] = jax.nn.sigmoid(a0_ref[...] + _dot(al, a2_ref[...]))
    go_ref[...] = _dot(gl, g2_ref[...])


def _rwkv_prep(zrkv, zlora, srkv, slora, murkv, mulora, w0, w2, a0, a2, g2, *, n, stride):
    nb, t, _ = zrkv.shape
    has_state = srkv is not None
    p = max(SUBLANES_V7X, stride)
    tile = lambda c: pl.BlockSpec((None, n, c), lambda i, j: (i, j, 0))
    state = lambda c: pl.BlockSpec((None, p, c), lambda i, j: (i, 0, 0))
    params = (murkv, mulora, w0, w2, a0, a2, g2)
    ins = [zrkv, zlora] + ([srkv, slora] if has_state else []) + list(params)
    in_specs = ([tile(D_RKV), tile(LORA_PAD)] + ([state(D_RKV), state(LORA_PAD)] if has_state else [])
                + [_const_spec(a.shape) for a in params])
    out = jax.ShapeDtypeStruct((nb, t, D_MODEL), F32)
    return pl.pallas_call(
        functools.partial(_prep_kernel, n=n, stride=stride, has_state=has_state),
        grid=(nb, t // n),
        in_specs=in_specs,
        out_specs=[tile(D_MODEL)] * 6,
        out_shape=[out] * 6,
        scratch_shapes=[pltpu.VMEM((p + n, D_RKV), F32), pltpu.VMEM((p + n, LORA_PAD), F32)],
        compiler_params=_cparams(2),
        name="rwkv_prep",
    )(*ins)


def _unit_lower_inverse(low, levels):
    eye, first, offs = levels
    t = jnp.where(eye, 1.0, 0.0) + jnp.where(first, low, 0.0)
    for m in offs:
        a = _dot(jnp.where(m, low, 0.0).astype(BF16), t.astype(BF16))
        t = t + _dot(t.astype(BF16), a.astype(BF16))
    return t


def _wkv_kernel(*refs, chunk, n_chunks, has_state):
    if has_state:
        (r_ref, k_ref, v_ref, lw_ref, alr_ref, go_ref, s0_ref, kk_ref, ka_ref, rk_ref, lng_ref, lnb_ref,
         y_ref, sout_ref, s_scr) = refs
    else:
        (r_ref, k_ref, v_ref, lw_ref, alr_ref, go_ref, kk_ref, ka_ref, rk_ref, lng_ref, lnb_ref,
         y_ref, sout_ref, s_scr) = refs
    c = chunk

    @pl.when(pl.program_id(1) == 0)
    def _():
        if has_state:
            s_scr[...] = s0_ref[...]
        else:
            s_scr[...] = jnp.zeros_like(s_scr)

    r = r_ref[...]
    k = k_ref[...]
    v = v_ref[...]
    lw = lw_ref[...]
    alr = alr_ref[...]
    go = go_ref[...]
    kkraw = k * kk_ref[...]
    kmod = k * (1.0 + (alr - 1.0) * ka_ref[...])

    row = lax.broadcasted_iota(jnp.int32, (c, c), 0)
    col = lax.broadcasted_iota(jnp.int32, (c, c), 1)
    incl = row >= col
    strict = row > col
    tri = jnp.where(incl, 1.0, 0.0).astype(BF16)
    hi = lw.astype(BF16)
    rem = lw - hi.astype(F32)
    mid = rem.astype(BF16)
    lo = (rem - mid.astype(F32)).astype(BF16)
    cum = _dot(tri, hi) + _dot(tri, mid) + _dot(tri, lo)
    w_incl = jnp.exp(cum)
    w_excl = jnp.exp(cum - lw)
    w_inv = jnp.exp(-cum)
    w_last = w_incl[c - 1:c, :]
    r_t = r * w_incl
    k_t = kmod * w_inv
    rk = r * kmod * rk_ref[...]

    sh = lambda x, s: lax.shift_right_logical(x, jnp.int32(s))
    eye = row == col
    first = strict & (sh(row, 1) == sh(col, 1))
    offs = []
    lg = 1
    while (1 << lg) < c:
        offs.append((sh(row, lg + 1) == sh(col, lg + 1))
                    & ((sh(row, lg) & 1) == 1) & ((sh(col, lg) & 1) == 0))
        lg += 1
    levels = (eye, first, offs)

    nt = (((1,), (1,)), ((), ()))
    tn = (((0,), (0,)), ((), ()))
    for h in range(N_HEADS):
        sl = slice(h * HEAD, (h + 1) * HEAD)
        kkr = kkraw[:, sl]
        nrm = jnp.sqrt(jnp.sum(kkr * kkr, axis=-1, keepdims=True))
        kk = kkr / jnp.maximum(nrm, 1e-12)
        a_t = -kk * w_excl[:, sl]
        b_t = kk * alr[:, sl] * w_inv[:, sl]
        vh = v[:, sl]
        s_h = s_scr[h]
        lhs = jnp.concatenate([a_t, r_t[:, sl]], axis=0).astype(BF16)
        rhs = jnp.concatenate([b_t, k_t[:, sl]], axis=0)
        pm = lax.dot_general(lhs, rhs.astype(BF16), nt, preferred_element_type=F32)
        gm = lax.dot_general(lhs, s_h.astype(BF16), nt, preferred_element_type=F32)
        t_inv = _unit_lower_inverse(jnp.where(strict, pm[:c, :c], 0.0), levels)
        rhs_u = gm[:c] + _dot(jnp.where(strict, pm[:c, c:], 0.0).astype(BF16), vh.astype(BF16))
        u = _dot(t_inv.astype(BF16), rhs_u.astype(BF16))
        uv = jnp.concatenate([u, vh], axis=0).astype(BF16)
        m2 = jnp.concatenate([jnp.where(incl, pm[c:, :c], 0.0), jnp.where(incl, pm[c:, c:], 0.0)], axis=1)
        y = gm[c:] + _dot(m2.astype(BF16), uv)
        wl_h = w_last[:, sl]
        s_scr[h] = s_h * wl_h + lax.dot_general(uv, (rhs * wl_h).astype(BF16), tn, preferred_element_type=F32)

        mu = jnp.mean(y, axis=-1, keepdims=True)
        yc = y - mu
        var = jnp.mean(yc * yc, axis=-1, keepdims=True)
        yn = yc * lax.rsqrt(var + GN_EPS) * lng_ref[:, sl] + lnb_ref[:, sl]
        bonus = jnp.sum(rk[:, sl], axis=-1, keepdims=True) * vh
        y_ref[:, sl] = (yn + bonus) * go[:, sl]

    @pl.when(pl.program_id(1) == n_chunks - 1)
    def _():
        sout_ref[...] = s_scr[...]


def _wkv(r, k, v, lw, alr, go, s0, kk, ka, rk, lng, lnb, *, chunk):
    nb, t, _ = r.shape
    has_state = s0 is not None
    n_chunks = t // chunk
    tile = pl.BlockSpec((None, chunk, D_MODEL), lambda i, j: (i, j, 0))
    sspec = pl.BlockSpec((None, N_HEADS, HEAD, HEAD), lambda i, j: (i, 0, 0, 0))
    params = (kk, ka, rk, lng, lnb)
    ins = [r, k, v, lw, alr, go] + ([s0] if has_state else []) + list(params)
    in_specs = [tile] * 6 + ([sspec] if has_state else []) + [_const_spec(a.shape) for a in params]
    return pl.pallas_call(
        functools.partial(_wkv_kernel, chunk=chunk, n_chunks=n_chunks, has_state=has_state),
        grid=(nb, n_chunks),
        in_specs=in_specs,
        out_specs=[tile, sspec],
        out_shape=[jax.ShapeDtypeStruct((nb, t, D_MODEL), F32),
                   jax.ShapeDtypeStruct((nb, N_HEADS, HEAD, HEAD), F32)],
        scratch_shapes=[pltpu.VMEM((N_HEADS, HEAD, HEAD), F32)],
        compiler_params=_cparams(2),
        name="wkv",
    )(*ins)


def _mix_kernel(x_ref, c_ref, y_ref, gate_ref, wco_ref, wro_ref, wmo_ref, o_ref):
    ya = _dot(c_ref[...], wco_ref[...])
    yb = _dot(y_ref[...].astype(BF16), wro_ref[...])
    g = gate_ref[...]
    m = g[:, :D_MODEL] * ya + g[:, D_MODEL:] * yb
    o_ref[...] = x_ref[...] + _dot(m.astype(BF16), wmo_ref[...])


def _mix(x, c, y, gate, wco, wro, wmo):
    rows = x.shape[0]
    n = ROW_TILE
    row = lambda w: pl.BlockSpec((n, w), lambda i: (i, 0))
    return pl.pallas_call(
        _mix_kernel,
        grid=(rows // n,),
        in_specs=[row(D_MODEL), row(D_CONV), row(D_MODEL), row(2 * D_MODEL),
                  _const_spec(wco.shape), _const_spec(wro.shape), _const_spec(wmo.shape)],
        out_specs=row(D_MODEL),
        out_shape=jax.ShapeDtypeStruct((rows, D_MODEL), F32),
        compiler_params=_cparams(1),
        name="mix",
    )(x, c, y, gate, wco, wro, wmo)


def _ffn_up_kernel(x_ref, g_ref, w_ref, o_ref):
    o_ref[...] = _dot(_rmsnorm(x_ref[...], g_ref[...]).astype(BF16), w_ref[...])


def _ffn_up(x, g, w):
    rows = x.shape[0]
    n = ROW_TILE
    return pl.pallas_call(
        _ffn_up_kernel,
        grid=(rows // n,),
        in_specs=[pl.BlockSpec((n, D_MODEL), lambda i: (i, 0)), _const_spec(g.shape), _const_spec(w.shape)],
        out_specs=pl.BlockSpec((n, 2 * D_FF), lambda i: (i, 0)),
        out_shape=jax.ShapeDtypeStruct((rows, 2 * D_FF), F32),
        compiler_params=_cparams(1),
        name="ffn_up",
    )(x, g, w)


GELU_C0 = 0.7978845608028654
GELU_C1 = 0.044715


def _ffn_down_kernel(*refs, n, stride, has_hist):
    if has_hist:
        up_ref, x_ref, hist_ref, w_ref, b_ref, wd_ref, o_ref, full_scr = refs
    else:
        up_ref, x_ref, w_ref, b_ref, wd_ref, o_ref, full_scr = refs
    hist_rows = (FFN_CONV_W - 1) * stride
    p = max(SUBLANES_V7X, hist_rows)

    @pl.when(pl.program_id(1) == 0)
    def _():
        if has_hist:
            full_scr[0:p, :] = hist_ref[...]
        else:
            full_scr[0:p, :] = jnp.zeros((p, 2 * D_FF), F32)

    up = up_ref[...]
    full_scr[p:p + n, :] = up
    w = w_ref[...]
    cu = (b_ref[...] + w[0:1, :] * full_scr[p - 2 * stride:p - 2 * stride + n, :]
          + w[1:2, :] * full_scr[p - stride:p - stride + n, :] + w[2:3, :] * up)
    full_scr[0:p, :] = full_scr[n:n + p, :]
    a = cu[:, :D_FF]
    gelu = 0.5 * a * (1.0 + jnp.tanh(GELU_C0 * (a + GELU_C1 * (a * a * a))))
    f = (gelu * cu[:, D_FF:]).astype(BF16)
    o_ref[...] = x_ref[...] + _dot(f, wd_ref[...])


def _ffn_down(up, x, hist, w, b, wd, *, n, stride):
    nb, t, _ = up.shape
    has_hist = hist is not None
    p = max(SUBLANES_V7X, (FFN_CONV_W - 1) * stride)
    tile = lambda c: pl.BlockSpec((None, n, c), lambda i, j: (i, j, 0))
    hspec = pl.BlockSpec((None, p, 2 * D_FF), lambda i, j: (i, 0, 0))
    ins = [up, x] + ([hist] if has_hist else []) + [w, b, wd]
    in_specs = ([tile(2 * D_FF), tile(D_MODEL)] + ([hspec] if has_hist else [])
                + [_const_spec(a.shape) for a in (w, b, wd)])
    return pl.pallas_call(
        functools.partial(_ffn_down_kernel, n=n, stride=stride, has_hist=has_hist),
        grid=(nb, t // n),
        in_specs=in_specs,
        out_specs=tile(D_MODEL),
        out_shape=jax.ShapeDtypeStruct((nb, t, D_MODEL), F32),
        scratch_shapes=[pltpu.VMEM((p + n, 2 * D_FF), F32)],
        compiler_params=_cparams(2),
        name="ffn_down",
    )(*ins)


def _final_norm_kernel(x_ref, g_ref, o_ref):
    o_ref[...] = _rmsnorm(x_ref[...], g_ref[...])


def _final_norm(x, g):
    rows = x.shape[0]
    n = ROW_TILE
    return pl.pallas_call(
        _final_norm_kernel,
        grid=(rows // n,),
        in_specs=[pl.BlockSpec((n, D_MODEL), lambda i: (i, 0)), _const_spec(g.shape)],
        out_specs=pl.BlockSpec((n, D_MODEL), lambda i: (i, 0)),
        out_shape=jax.ShapeDtypeStruct((rows, D_MODEL), F32),
        compiler_params=_cparams(1),
        name="final_norm",
    )(x, g)


def _pad_cols(a, width):
    return jnp.pad(a, ((0, 0), (0, width - a.shape[1])))


def _pad_rows(a, height):
    return jnp.pad(a, ((0, height - a.shape[0]), (0, 0)))


def _regroup_lora_cols(a):
    wl = a[:, 0:LORA_DECAY]
    al = a[:, LORA_DECAY:LORA_DECAY + LORA_AAA]
    gl = a[:, LORA_DECAY + LORA_AAA:]
    return jnp.concatenate([_pad_cols(wl, LANES_V7X), _pad_cols(al, LANES_V7X),
                            _pad_cols(gl, LORA_PAD - LORA_G_OFF)], axis=1)


def _ungroup_lora_cols(a):
    return jnp.concatenate([a[..., LORA_W_OFF:LORA_W_OFF + LORA_DECAY], a[..., LORA_A_OFF:LORA_A_OFF + LORA_AAA],
                            a[..., LORA_G_OFF:LORA_G_OFF + LORA_GATE]], axis=-1)


def _layer_weights(l, norm_mix_g, w_in, conv_dw_w, conv_dw_b, conv_ln_g, conv_ln_b, w_conv_out,
                   rw_mu, rw_w0, rw_w2, rw_a0, rw_a2, rw_g2, rw_k_k, rw_k_a, rw_r_k, rw_ln_g, rw_ln_b, w_rw_out,
                   w_mix_out, norm_ffn_g, w_up, ffn_dw_w, ffn_dw_b, w_down):
    row = lambda a: a[l].reshape(1, -1)
    wi = w_in[l]
    o = 2 * D_CONV
    return dict(
        norm_mix_g=row(norm_mix_g),
        wc=wi[:, :o].astype(BF16),
        wrkv=wi[:, o:o + D_RKV].astype(BF16),
        wlora=_regroup_lora_cols(wi[:, o + D_RKV:o + D_RW_IN]).astype(BF16),
        wg=wi[:, o + D_RW_IN:].astype(BF16),
        conv_w=conv_dw_w[l], conv_b=row(conv_dw_b), conv_lg=row(conv_ln_g), conv_lb=row(conv_ln_b),
        wco=w_conv_out[l].astype(BF16),
        murkv=row(rw_mu)[:, :D_RKV], mulora=_regroup_lora_cols(row(rw_mu)[:, D_RKV:]),
        w0=row(rw_w0), w2=_pad_rows(rw_w2[l], LANES_V7X).astype(BF16),
        a0=row(rw_a0), a2=_pad_rows(rw_a2[l], LANES_V7X).astype(BF16),
        g2=_pad_rows(rw_g2[l], LORA_PAD - LORA_G_OFF).astype(BF16),
        kk=row(rw_k_k), ka=row(rw_k_a), rk=row(rw_r_k), lng=row(rw_ln_g), lnb=row(rw_ln_b),
        wro=w_rw_out[l].astype(BF16), wmo=w_mix_out[l].astype(BF16),
        norm_ffn_g=row(norm_ffn_g), wup=w_up[l].astype(BF16),
        ffn_w=ffn_dw_w[l], ffn_b=row(ffn_dw_b), wdown=w_down[l].astype(BF16),
    )


def _layer(x, p, *, nb, stride, chunk, conv_n, conv_hist, shift_state, wkv_state, ffn_hist, wkv_layout):
    rows = x.shape[0]
    t = rows // nb
    u, zrkv, zlora, gate = _inproj(x, p["norm_mix_g"], p["wc"], p["wrkv"], p["wlora"], p["wg"])
    g3 = lambda a: a.reshape(nb, t, a.shape[-1])
    c, new_conv = _conv_branch(g3(u), conv_hist, p["conv_w"], p["conv_b"], p["conv_lg"], p["conv_lb"],
                               n=conv_n, stride=stride)
    srkv, slora = shift_state if shift_state is not None else (None, None)
    r, k, v, lw, alr, go = _rwkv_prep(g3(zrkv), g3(zlora), srkv, slora, p["murkv"], p["mulora"],
                                      p["w0"], p["w2"], p["a0"], p["a2"], p["g2"], n=ROW_TILE, stride=stride)
    to_seq, from_seq = wkv_layout
    y, new_wkv = _wkv(*(to_seq(a) for a in (r, k, v, lw, alr, go)), wkv_state,
                      p["kk"], p["ka"], p["rk"], p["lng"], p["lnb"], chunk=chunk)
    y = from_seq(y)
    x1 = _mix(x, c.reshape(rows, D_CONV), y.reshape(rows, D_MODEL), gate, p["wco"], p["wro"], p["wmo"])
    up = _ffn_up(x1, p["norm_ffn_g"], p["wup"])
    x2 = _ffn_down(g3(up), g3(x1), ffn_hist, p["ffn_w"], p["ffn_b"], p["wdown"], n=ROW_TILE, stride=stride)
    return x2.reshape(rows, D_MODEL), new_conv, zrkv, zlora, new_wkv, up


def kernel(x_prompt, x_sample, state_conv, state_shift, state_wkv, state_ffn,
           norm_mix_g, w_in, conv_dw_w, conv_dw_b, conv_ln_g, conv_ln_b, w_conv_out,
           rw_mu, rw_w0, rw_w2, rw_a0, rw_a2, rw_g2, rw_k_k, rw_k_a, rw_r_k, rw_ln_g, rw_ln_b, w_rw_out,
           w_mix_out, norm_ffn_g, w_up, ffn_dw_w, ffn_dw_b, w_down, norm_final_g):
    params = (norm_mix_g, w_in, conv_dw_w, conv_dw_b, conv_ln_g, conv_ln_b, w_conv_out,
              rw_mu, rw_w0, rw_w2, rw_a0, rw_a2, rw_g2, rw_k_k, rw_k_a, rw_r_k, rw_ln_g, rw_ln_b, w_rw_out,
              w_mix_out, norm_ffn_g, w_up, ffn_dw_w, ffn_dw_b, w_down)
    depth = w_in.shape[0]
    b_p, t_p, _ = x_prompt.shape
    b_s, t_s, _ = x_sample.shape
    final_g = norm_final_g.reshape(1, -1)

    xp = x_prompt.reshape(b_p * t_p, D_MODEL)
    xs = jnp.transpose(x_sample, (1, 0, 2)).reshape(t_s * b_s, D_MODEL)

    ident = lambda a: a
    prompt_layout = (ident, ident)

    def sample_to_seq(a):
        a = jnp.transpose(a.reshape(t_s, b_s, a.shape[-1]), (1, 0, 2))
        return jnp.pad(a, ((0, 0), (0, SAMPLE_CHUNK - t_s), (0, 0)))

    def sample_from_seq(a):
        return jnp.transpose(a[:, :t_s], (1, 0, 2)).reshape(1, t_s * b_s, a.shape[-1])

    p_conv, p_shift, p_wkv, p_ffn = [], [], [], []
    s_conv, s_shift, s_wkv, s_ffn = [], [], [], []
    for l in range(depth):
        p = _layer_weights(l, *params)

        xp, nc, zrkv, zlora, nw, up = _layer(
            xp, p, nb=b_p, stride=1, chunk=PROMPT_CHUNK, conv_n=2 * ROW_TILE, conv_hist=None, shift_state=None,
            wkv_state=None, ffn_hist=None, wkv_layout=prompt_layout)
        p_conv.append(nc)
        last = lambda a: a.reshape(b_p, t_p, a.shape[-1])[:, t_p - 1]
        p_shift.append(jnp.concatenate([last(zrkv), _ungroup_lora_cols(last(zlora))], axis=-1))
        p_wkv.append(nw)
        p_ffn.append(up.reshape(b_p, t_p, 2 * D_FF)[:, t_p - (FFN_CONV_W - 1):])

        tm = lambda a: jnp.transpose(a, (1, 0, 2)).reshape(1, a.shape[1] * b_s, a.shape[2])
        sh = state_shift[l]
        xs, nc, zrkv, zlora, nw, up = _layer(
            xs, p, nb=1, stride=b_s, chunk=SAMPLE_CHUNK, conv_n=t_s * b_s, conv_hist=tm(state_conv[l]),
            shift_state=(sh[:, :D_RKV].reshape(1, b_s, D_RKV),
                         _regroup_lora_cols(sh[:, D_RKV:]).reshape(1, b_s, LORA_PAD)),
            wkv_state=state_wkv[l], ffn_hist=tm(state_ffn[l]), wkv_layout=(sample_to_seq, sample_from_seq))
        s_conv.append(jnp.transpose(nc.reshape(CONV_W - 1, b_s, D_CONV), (1, 0, 2)))
        lastrows = lambda a: a[(t_s - 1) * b_s:]
        s_shift.append(jnp.concatenate([lastrows(zrkv), _ungroup_lora_cols(lastrows(zlora))], axis=-1))
        s_wkv.append(nw)
        s_ffn.append(jnp.transpose(up[(t_s - (FFN_CONV_W - 1)) * b_s:].reshape(FFN_CONV_W - 1, b_s, 2 * D_FF),
                                   (1, 0, 2)))

    y_prompt = _final_norm(xp, final_g).reshape(b_p, t_p, D_MODEL)
    y_sample = jnp.transpose(_final_norm(xs, final_g).reshape(t_s, b_s, D_MODEL), (1, 0, 2))
    return (y_prompt, y_sample, jnp.stack(p_conv), jnp.stack(p_shift), jnp.stack(p_wkv), jnp.stack(p_ffn),
            jnp.stack(s_conv), jnp.stack(s_shift), jnp.stack(s_wkv), jnp.stack(s_ffn))
```

```python
import functools

import jax
import jax.numpy as jnp
from jax import lax
from jax.experimental import pallas as pl
from jax.experimental.pallas import tpu as pltpu

F32 = jnp.float32
BF16 = jnp.bfloat16

D_MODEL = 1024
D_CONV = D_MODEL // 2
CONV_W = 31
HEAD = 64
N_HEADS = D_MODEL // HEAD
LORA_DECAY = 64
LORA_AAA = 64
LORA_GATE = 160
D_FF = 3 * D_MODEL
FFN_CONV_W = 3
D_RKV = 3 * D_MODEL
D_RW_IN = D_RKV + LORA_DECAY + LORA_AAA + LORA_GATE
RMS_EPS = 1e-6
LN_EPS = 1e-5
GN_EPS = 64e-5

SUBLANES_V7X = 8
LANES_V7X = 128
VMEM_LIMIT_BYTES_V7X = 56 * 1024 * 1024

LORA_PAD = 4 * LANES_V7X
LORA_W_OFF, LORA_A_OFF, LORA_G_OFF = 0, LANES_V7X, 2 * LANES_V7X

ROW_TILE = 128
PROMPT_CHUNK = 64
SAMPLE_CHUNK = 8
PROMPT_GROUP = 2
SAMPLE_GROUP = 4


def _cparams(n_grid):
    return pltpu.CompilerParams(dimension_semantics=("arbitrary",) * n_grid,
                                vmem_limit_bytes=VMEM_LIMIT_BYTES_V7X)


def _const_spec(shape):
    return pl.BlockSpec(shape, lambda *_: (0,) * len(shape), pipeline_mode=pl.Buffered(1))


def _dot(a, b):
    return jnp.dot(a, b, preferred_element_type=F32)


def _rmsnorm(x, g):
    return x * lax.rsqrt(jnp.mean(x * x, axis=-1, keepdims=True) + RMS_EPS) * g


def _inproj_kernel(x_ref, g_ref, wc_ref, wrkv_ref, wlora_ref, wg_ref,
                   u_ref, zrkv_ref, zlora_ref, gate_ref):
    hb = _rmsnorm(x_ref[...], g_ref[...]).astype(BF16)
    zc = _dot(hb, wc_ref[...])
    u_ref[...] = zc[:, :D_CONV] * jax.nn.sigmoid(zc[:, D_CONV:])
    zrkv_ref[...] = _dot(hb, wrkv_ref[...])
    zlora_ref[...] = _dot(hb, wlora_ref[...])
    gate_ref[...] = jax.nn.sigmoid(_dot(hb, wg_ref[...]))


def _inproj(x, g, wc, wrkv, wlora, wg):
    rows = x.shape[0]
    n = ROW_TILE
    row = lambda c: pl.BlockSpec((n, c), lambda i: (i, 0))
    return pl.pallas_call(
        _inproj_kernel,
        grid=(rows // n,),
        in_specs=[row(D_MODEL), _const_spec(g.shape), _const_spec(wc.shape), _const_spec(wrkv.shape),
                  _const_spec(wlora.shape), _const_spec(wg.shape)],
        out_specs=[row(D_CONV), row(D_RKV), row(LORA_PAD), row(2 * D_MODEL)],
        out_shape=[jax.ShapeDtypeStruct((rows, D_CONV), F32), jax.ShapeDtypeStruct((rows, D_RKV), F32),
                   jax.ShapeDtypeStruct((rows, LORA_PAD), F32), jax.ShapeDtypeStruct((rows, 2 * D_MODEL), F32)],
        compiler_params=_cparams(1),
        name="inproj",
    )(x, g, wc, wrkv, wlora, wg)


CONV_ROW_BLOCK = 64


def _conv_kernel(*refs, n, n_steps, stride, has_hist):
    if has_hist:
        u_ref, hist_ref, w_ref, b_ref, lg_ref, lb_ref, c_ref, newhist_ref, full_scr = refs
    else:
        u_ref, w_ref, b_ref, lg_ref, lb_ref, c_ref, newhist_ref, full_scr = refs
    hist_rows = (CONV_W - 1) * stride
    pad = (-hist_rows) % SUBLANES_V7X
    base = pad + hist_rows

    @pl.when(pl.program_id(1) == 0)
    def _():
        if has_hist:
            if pad:
                full_scr[0:pad, :] = jnp.zeros((pad, D_CONV), F32)
            full_scr[pad:base, :] = hist_ref[...]
        else:
            full_scr[0:base, :] = jnp.zeros((base, D_CONV), F32)

    full_scr[base:base + n, :] = u_ref[...]
    w = w_ref[...]
    for rb in range(n // CONV_ROW_BLOCK):
        r0 = rb * CONV_ROW_BLOCK
        acc = jnp.broadcast_to(b_ref[...], (CONV_ROW_BLOCK, D_CONV))
        for j in range(CONV_W):
            s = pad + r0 + j * stride
            acc = acc + w[j:j + 1, :] * full_scr[s:s + CONV_ROW_BLOCK, :]
        mu = jnp.mean(acc, axis=-1, keepdims=True)
        xc = acc - mu
        var = jnp.mean(xc * xc, axis=-1, keepdims=True)
        y = xc * lax.rsqrt(var + LN_EPS) * lg_ref[...] + lb_ref[...]
        c_ref[r0:r0 + CONV_ROW_BLOCK, :] = (y * jax.nn.sigmoid(y)).astype(BF16)

    newhist_ref[...] = full_scr[pad + n:base + n, :]
    if n_steps > 1:
        full_scr[0:base, :] = full_scr[n:n + base, :]


def _conv_branch(u, hist, w, b, lg, lb, *, n, stride):
    nb, t, _ = u.shape
    has_hist = hist is not None
    hist_rows = (CONV_W - 1) * stride
    base = hist_rows + (-hist_rows) % SUBLANES_V7X
    tile = pl.BlockSpec((None, n, D_CONV), lambda i, j: (i, j, 0))
    hspec = pl.BlockSpec((None, hist_rows, D_CONV), lambda i, j: (i, 0, 0))
    ins = [u] + ([hist] if has_hist else []) + [w, b, lg, lb]
    in_specs = [tile] + ([hspec] if has_hist else []) + [_const_spec(a.shape) for a in (w, b, lg, lb)]
    return pl.pallas_call(
        functools.partial(_conv_kernel, n=n, n_steps=t // n, stride=stride, has_hist=has_hist),
        grid=(nb, t // n),
        in_specs=in_specs,
        out_specs=[tile, hspec],
        out_shape=[jax.ShapeDtypeStruct((nb, t, D_CONV), BF16), jax.ShapeDtypeStruct((nb, hist_rows, D_CONV), F32)],
        scratch_shapes=[pltpu.VMEM((base + n, D_CONV), F32)],
        compiler_params=_cparams(2),
        name="conv_branch",
    )(*ins)


def _prep_kernel(*refs, n, stride, has_state):
    if has_state:
        (zrkv_ref, zlora_ref, srkv_ref, slora_ref, murkv_ref, mulora_ref, w0_ref, w2_ref, a0_ref, a2_ref, g2_ref,
         r_ref, k_ref, v_ref, lw_ref, alr_ref, go_ref, frkv_scr, flora_scr) = refs
    else:
        (zrkv_ref, zlora_ref, murkv_ref, mulora_ref, w0_ref, w2_ref, a0_ref, a2_ref, g2_ref,
         r_ref, k_ref, v_ref, lw_ref, alr_ref, go_ref, frkv_scr, flora_scr) = refs
    p = max(SUBLANES_V7X, stride)

    @pl.when(pl.program_id(1) == 0)
    def _():
        if has_state:
            frkv_scr[0:p, :] = srkv_ref[...]
            flora_scr[0:p, :] = slora_ref[...]
        else:
            frkv_scr[0:p, :] = jnp.zeros((p, D_RKV), F32)
            flora_scr[0:p, :] = jnp.zeros((p, LORA_PAD), F32)

    zrkv = zrkv_ref[...]
    zlora = zlora_ref[...]
    frkv_scr[p:p + n, :] = zrkv
    flora_scr[p:p + n, :] = zlora
    xs = zrkv + (frkv_scr[p - stride:p - stride + n, :] - zrkv) * murkv_ref[...]
    xl = zlora + (flora_scr[p - stride:p - stride + n, :] - zlora) * mulora_ref[...]
    frkv_scr[0:p, :] = frkv_scr[n:n + p, :]
    flora_scr[0:p, :] = flora_scr[n:n + p, :]

    r_ref[...] = xs[:, 0:D_MODEL]
    k_ref[...] = xs[:, D_MODEL:2 * D_MODEL]
    v_ref[...] = xs[:, 2 * D_MODEL:3 * D_MODEL]

    wl = jnp.tanh(xl[:, LORA_W_OFF:LORA_W_OFF + LANES_V7X]).astype(BF16)
    al = xl[:, LORA_A_OFF:LORA_A_OFF + LANES_V7X].astype(BF16)
    gl = jax.nn.sigmoid(xl[:, LORA_G_OFF:LORA_PAD]).astype(BF16)
    t = -(w0_ref[...] + _dot(wl, w2_ref[...]))
    softplus = jnp.maximum(t, 0.0) + jnp.log(1.0 + jnp.exp(-jnp.abs(t)))
    lw_ref[...] = -jnp.exp(-softplus - 0.5)
    alr_ref[...] = jax.nn.sigmoid(a0_ref[...] + _dot(al, a2_ref[...]))
    go_ref[...] = _dot(gl, g2_ref[...])


def _rwkv_prep(zrkv, zlora, srkv, slora, murkv, mulora, w0, w2, a0, a2, g2, *, n, stride):
    nb, t, _ = zrkv.shape
    has_state = srkv is not None
    p = max(SUBLANES_V7X, stride)
    tile = lambda c: pl.BlockSpec((None, n, c), lambda i, j: (i, j, 0))
    state = lambda c: pl.BlockSpec((None, p, c), lambda i, j: (i, 0, 0))
    params = (murkv, mulora, w0, w2, a0, a2, g2)
    ins = [zrkv, zlora] + ([srkv, slora] if has_state else []) + list(params)
    in_specs = ([tile(D_RKV), tile(LORA_PAD)] + ([state(D_RKV), state(LORA_PAD)] if has_state else [])
                + [_const_spec(a.shape) for a in params])
    out = jax.ShapeDtypeStruct((nb, t, D_MODEL), F32)
    return pl.pallas_call(
        functools.partial(_prep_kernel, n=n, stride=stride, has_state=has_state),
        grid=(nb, t // n),
        in_specs=in_specs,
        out_specs=[tile(D_MODEL)] * 6,
        out_shape=[out] * 6,
        scratch_shapes=[pltpu.VMEM((p + n, D_RKV), F32), pltpu.VMEM((p + n, LORA_PAD), F32)],
        compiler_params=_cparams(2),
        name="rwkv_prep",
    )(*ins)


def _wkv_kernel(*refs, chunk, n_chunks, group, has_state):
    if has_state:
        (r_ref, k_ref, v_ref, lw_ref, alr_ref, go_ref, s0_ref, kk_ref, ka_ref, rk_ref, lng_ref, lnb_ref,
         y_ref, sout_ref, s_scr) = refs
    else:
        (r_ref, k_ref, v_ref, lw_ref, alr_ref, go_ref, kk_ref, ka_ref, rk_ref, lng_ref, lnb_ref,
         y_ref, sout_ref, s_scr) = refs
    c = chunk
    transposed_state = not has_state

    @pl.when(pl.program_id(1) == 0)
    def _():
        if has_state:
            s_scr[...] = s0_ref[...]
        else:
            s_scr[...] = jnp.zeros_like(s_scr)

    row = lax.broadcasted_iota(jnp.int32, (c, c), 0)
    col = lax.broadcasted_iota(jnp.int32, (c, c), 1)
    incl = row >= col
    strict = row > col
    tri = jnp.where(incl, 1.0, 0.0).astype(BF16)
    sh = lambda x, s: lax.shift_right_logical(x, jnp.int32(s))
    eye = row == col
    first = strict & (sh(row, 1) == sh(col, 1))
    offs = []
    lg = 1
    while (1 << lg) < c:
        offs.append((sh(row, lg + 1) == sh(col, lg + 1))
                    & ((sh(row, lg) & 1) == 1) & ((sh(col, lg) & 1) == 0))
        lg += 1

    seg_w = LANES_V7X * 2
    srow = lax.broadcasted_iota(jnp.int32, (seg_w, seg_w), 0)
    scol = lax.broadcasted_iota(jnp.int32, (seg_w, seg_w), 1)
    head_lg = HEAD.bit_length() - 1
    blockdiag = jnp.where(sh(srow, head_lg) == sh(scol, head_lg), 1.0, 0.0).astype(BF16)

    def head_sums(xs):
        his = [x.astype(BF16) for x in xs]
        los = [(x - hh.astype(F32)).astype(BF16) for x, hh in zip(xs, his)]
        outs = []
        for hh, ll in zip(his, los):
            cols = [_dot(hh[:, q:q + seg_w], blockdiag) + _dot(ll[:, q:q + seg_w], blockdiag)
                    for q in range(0, D_MODEL, seg_w)]
            outs.append(jnp.concatenate(cols, axis=1))
        return outs

    gs = range(group)
    r = [r_ref[g] for g in gs]
    k = [k_ref[g] for g in gs]
    v = [v_ref[g] for g in gs]
    lw = [lw_ref[g] for g in gs]
    alr = [alr_ref[g] for g in gs]
    kkraw = [k[g] * kk_ref[...] for g in gs]
    kmod = [k[g] * (1.0 + (alr[g] - 1.0) * ka_ref[...]) for g in gs]
    hi = [lw[g].astype(BF16) for g in gs]
    rem = [lw[g] - hi[g].astype(F32) for g in gs]
    mid = [rem[g].astype(BF16) for g in gs]
    lo = [(rem[g] - mid[g].astype(F32)).astype(BF16) for g in gs]
    cum = [_dot(tri, hi[g]) + _dot(tri, mid[g]) + _dot(tri, lo[g]) for g in gs]
    nrm2 = head_sums([kkraw[g] * kkraw[g] for g in gs])
    w_incl = [jnp.exp(cum[g]) for g in gs]
    w_excl = [jnp.exp(cum[g] - lw[g]) for g in gs]
    w_inv = [jnp.exp(-cum[g]) for g in gs]
    w_last = [w_incl[g][c - 1:c, :] for g in gs]
    kk = [kkraw[g] / jnp.maximum(jnp.sqrt(nrm2[g]), 1e-12) for g in gs]
    a_t = [-kk[g] * w_excl[g] for g in gs]
    b_t = [kk[g] * alr[g] * w_inv[g] for g in gs]
    r_t = [r[g] * w_incl[g] for g in gs]
    k_t = [kmod[g] * w_inv[g] for g in gs]
    lhs_f = [jnp.concatenate([a_t[g], r_t[g]], axis=0).astype(BF16) for g in gs]
    rhs_f = [jnp.concatenate([b_t[g], k_t[g]], axis=0) for g in gs]
    rhs_b = [rhs_f[g].astype(BF16) for g in gs]
    rhs_w = [rhs_f[g] * w_last[g] for g in gs]
    v_b = [v[g].astype(BF16) for g in gs]
    if transposed_state:
        rhs_wt = [[rhs_w[g][:, q:q + LANES_V7X].T.astype(BF16) for q in range(0, D_MODEL, LANES_V7X)] for g in gs]
        decay = [[jnp.broadcast_to(w_last[g][:, q:q + LANES_V7X], (LANES_V7X, LANES_V7X)).T
                  for q in range(0, D_MODEL, LANES_V7X)] for g in gs]
    else:
        rhs_wb = [rhs_w[g].astype(BF16) for g in gs]

    chains = [(g, h) for g in gs for h in range(N_HEADS)]
    sl = lambda h: slice(h * HEAD, (h + 1) * HEAD)
    nt = (((1,), (1,)), ((), ()))
    tn = (((0,), (0,)), ((), ()))
    s_old = [s_scr[g, h] for g, h in chains]
    pm = [lax.dot_general(lhs_f[g][:, sl(h)], rhs_b[g][:, sl(h)], nt, preferred_element_type=F32)
          for g, h in chains]
    if transposed_state:
        gm = [_dot(lhs_f[g][:, sl(h)], s_old[i].astype(BF16)) for i, (g, h) in enumerate(chains)]
    else:
        gm = [lax.dot_general(lhs_f[g][:, sl(h)], s_old[i].astype(BF16), nt, preferred_element_type=F32)
              for i, (g, h) in enumerate(chains)]
    idx = range(len(chains))
    low = [jnp.where(strict, pm[i][:c, :c], 0.0) for i in idx]
    pakv = [_dot(jnp.where(strict, pm[i][:c, c:], 0.0).astype(BF16), v_b[g][:, sl(h)])
            for i, (g, h) in enumerate(chains)]
    t = [jnp.where(eye, 1.0, 0.0) + jnp.where(first, low[i], 0.0) for i in idx]
    for m in offs:
        tb = [t[i].astype(BF16) for i in idx]
        a = [_dot(jnp.where(m, low[i], 0.0).astype(BF16), tb[i]) for i in idx]
        t = [t[i] + _dot(tb[i], a[i].astype(BF16)) for i in idx]
    u = [_dot(t[i].astype(BF16), (gm[i][:c] + pakv[i]).astype(BF16)) for i in idx]
    uv = [jnp.concatenate([u[i].astype(BF16), v_b[g][:, sl(h)]], axis=0) for i, (g, h) in enumerate(chains)]
    m2 = [jnp.concatenate([jnp.where(incl, pm[i][c:, :c], 0.0), jnp.where(incl, pm[i][c:, c:], 0.0)],
                          axis=1).astype(BF16) for i in idx]
    y = [gm[i][c:] + _dot(m2[i], uv[i]) for i in idx]
    for i, (g, h) in enumerate(chains):
        if transposed_state:
            q, half = divmod(h, 2)
            rows = slice(half * HEAD, (half + 1) * HEAD)
            s_scr[g, h] = s_old[i] * decay[g][q][rows, :HEAD] + _dot(rhs_wt[g][q][rows, :], uv[i])
        else:
            s_scr[g, h] = (s_old[i] * w_last[g][:, sl(h)]
                           + lax.dot_general(uv[i], rhs_wb[g][:, sl(h)], tn, preferred_element_type=F32))

    y_f = [jnp.concatenate(y[g * N_HEADS:(g + 1) * N_HEADS], axis=1) for g in gs]
    mu = head_sums(y_f)
    yc = [y_f[g] - mu[g] * (1.0 / HEAD) for g in gs]
    var_bonus = head_sums([yc[g] * yc[g] for g in gs] + [r[g] * kmod[g] * rk_ref[...] for g in gs])
    for g in gs:
        yn = yc[g] * lax.rsqrt(var_bonus[g] * (1.0 / HEAD) + GN_EPS) * lng_ref[...] + lnb_ref[...]
        y_ref[g] = (yn + var_bonus[group + g] * v[g]) * go_ref[g]

    @pl.when(pl.program_id(1) == n_chunks - 1)
    def _():
        if transposed_state:
            hrow = lax.broadcasted_iota(jnp.int32, (HEAD, HEAD), 0)
            hcol = lax.broadcasted_iota(jnp.int32, (HEAD, HEAD), 1)
            ident = jnp.where(hrow == hcol, 1.0, 0.0).astype(BF16)
            xt = lambda p: lax.dot_general(p, ident, tn, preferred_element_type=F32)
            for g, h in chains:
                st = s_scr[g, h]
                p0 = st.astype(BF16)
                rem0 = st - p0.astype(F32)
                p1 = rem0.astype(BF16)
                p2 = (rem0 - p1.astype(F32)).astype(BF16)
                sout_ref[g, h] = xt(p0) + xt(p1) + xt(p2)
        else:
            sout_ref[...] = s_scr[...]


def _wkv(r, k, v, lw, alr, go, s0, kk, ka, rk, lng, lnb, *, chunk, group):
    nb, t, _ = r.shape
    has_state = s0 is not None
    n_chunks = t // chunk
    tile = pl.BlockSpec((group, chunk, D_MODEL), lambda i, j: (i, j, 0))
    sspec = pl.BlockSpec((group, N_HEADS, HEAD, HEAD), lambda i, j: (i, 0, 0, 0))
    params = (kk, ka, rk, lng, lnb)
    ins = [r, k, v, lw, alr, go] + ([s0] if has_state else []) + list(params)
    in_specs = [tile] * 6 + ([sspec] if has_state else []) + [_const_spec(a.shape) for a in params]
    return pl.pallas_call(
        functools.partial(_wkv_kernel, chunk=chunk, n_chunks=n_chunks, group=group, has_state=has_state),
        grid=(nb // group, n_chunks),
        in_specs=in_specs,
        out_specs=[tile, sspec],
        out_shape=[jax.ShapeDtypeStruct((nb, t, D_MODEL), F32),
                   jax.ShapeDtypeStruct((nb, N_HEADS, HEAD, HEAD), F32)],
        scratch_shapes=[pltpu.VMEM((group, N_HEADS, HEAD, HEAD), F32)],
        compiler_params=_cparams(2),
        name="wkv",
    )(*ins)


def _mix_kernel(x_ref, c_ref, y_ref, gate_ref, wco_ref, wro_ref, wmo_ref, o_ref):
    ya = _dot(c_ref[...], wco_ref[...])
    yb = _dot(y_ref[...].astype(BF16), wro_ref[...])
    g = gate_ref[...]
    m = g[:, :D_MODEL] * ya + g[:, D_MODEL:] * yb
    o_ref[...] = x_ref[...] + _dot(m.astype(BF16), wmo_ref[...])


def _mix(x, c, y, gate, wco, wro, wmo):
    rows = x.shape[0]
    n = ROW_TILE
    row = lambda w: pl.BlockSpec((n, w), lambda i: (i, 0))
    return pl.pallas_call(
        _mix_kernel,
        grid=(rows // n,),
        in_specs=[row(D_MODEL), row(D_CONV), row(D_MODEL), row(2 * D_MODEL),
                  _const_spec(wco.shape), _const_spec(wro.shape), _const_spec(wmo.shape)],
        out_specs=row(D_MODEL),
        out_shape=jax.ShapeDtypeStruct((rows, D_MODEL), F32),
        compiler_params=_cparams(1),
        name="mix",
    )(x, c, y, gate, wco, wro, wmo)


def _ffn_up_kernel(x_ref, g_ref, w_ref, o_ref):
    o_ref[...] = _dot(_rmsnorm(x_ref[...], g_ref[...]).astype(BF16), w_ref[...])


def _ffn_up(x, g, w):
    rows = x.shape[0]
    n = ROW_TILE
    return pl.pallas_call(
        _ffn_up_kernel,
        grid=(rows // n,),
        in_specs=[pl.BlockSpec((n, D_MODEL), lambda i: (i, 0)), _const_spec(g.shape), _const_spec(w.shape)],
        out_specs=pl.BlockSpec((n, 2 * D_FF), lambda i: (i, 0)),
        out_shape=jax.ShapeDtypeStruct((rows, 2 * D_FF), F32),
        compiler_params=_cparams(1),
        name="ffn_up",
    )(x, g, w)


GELU_C0 = 0.7978845608028654
GELU_C1 = 0.044715


def _ffn_down_kernel(*refs, n, stride, has_hist):
    if has_hist:
        up_ref, x_ref, hist_ref, w_ref, b_ref, wd_ref, o_ref, full_scr = refs
    else:
        up_ref, x_ref, w_ref, b_ref, wd_ref, o_ref, full_scr = refs
    hist_rows = (FFN_CONV_W - 1) * stride
    p = max(SUBLANES_V7X, hist_rows)

    @pl.when(pl.program_id(1) == 0)
    def _():
        if has_hist:
            full_scr[0:p, :] = hist_ref[...]
        else:
            full_scr[0:p, :] = jnp.zeros((p, 2 * D_FF), F32)

    up = up_ref[...]
    full_scr[p:p + n, :] = up
    w = w_ref[...]
    cu = (b_ref[...] + w[0:1, :] * full_scr[p - 2 * stride:p - 2 * stride + n, :]
          + w[1:2, :] * full_scr[p - stride:p - stride + n, :] + w[2:3, :] * up)
    full_scr[0:p, :] = full_scr[n:n + p, :]
    a = cu[:, :D_FF]
    gelu = 0.5 * a * (1.0 + jnp.tanh(GELU_C0 * (a + GELU_C1 * (a * a * a))))
    f = (gelu * cu[:, D_FF:]).astype(BF16)
    o_ref[...] = x_ref[...] + _dot(f, wd_ref[...])


def _ffn_down(up, x, hist, w, b, wd, *, n, stride):
    nb, t, _ = up.shape
    has_hist = hist is not None
    p = max(SUBLANES_V7X, (FFN_CONV_W - 1) * stride)
    tile = lambda c: pl.BlockSpec((None, n, c), lambda i, j: (i, j, 0))
    hspec = pl.BlockSpec((None, p, 2 * D_FF), lambda i, j: (i, 0, 0))
    ins = [up, x] + ([hist] if has_hist else []) + [w, b, wd]
    in_specs = ([tile(2 * D_FF), tile(D_MODEL)] + ([hspec] if has_hist else [])
                + [_const_spec(a.shape) for a in (w, b, wd)])
    return pl.pallas_call(
        functools.partial(_ffn_down_kernel, n=n, stride=stride, has_hist=has_hist),
        grid=(nb, t // n),
        in_specs=in_specs,
        out_specs=tile(D_MODEL),
        out_shape=jax.ShapeDtypeStruct((nb, t, D_MODEL), F32),
        scratch_shapes=[pltpu.VMEM((p + n, 2 * D_FF), F32)],
        compiler_params=_cparams(2),
        name="ffn_down",
    )(*ins)


def _final_norm_kernel(x_ref, g_ref, o_ref):
    o_ref[...] = _rmsnorm(x_ref[...], g_ref[...])


def _final_norm(x, g):
    rows = x.shape[0]
    n = ROW_TILE
    return pl.pallas_call(
        _final_norm_kernel,
        grid=(rows // n,),
        in_specs=[pl.BlockSpec((n, D_MODEL), lambda i: (i, 0)), _const_spec(g.shape)],
        out_specs=pl.BlockSpec((n, D_MODEL), lambda i: (i, 0)),
        out_shape=jax.ShapeDtypeStruct((rows, D_MODEL), F32),
        compiler_params=_cparams(1),
        name="final_norm",
    )(x, g)


def _pad_cols(a, width):
    return jnp.pad(a, ((0, 0), (0, width - a.shape[1])))


def _pad_rows(a, height):
    return jnp.pad(a, ((0, height - a.shape[0]), (0, 0)))


def _regroup_lora_cols(a):
    wl = a[:, 0:LORA_DECAY]
    al = a[:, LORA_DECAY:LORA_DECAY + LORA_AAA]
    gl = a[:, LORA_DECAY + LORA_AAA:]
    return jnp.concatenate([_pad_cols(wl, LANES_V7X), _pad_cols(al, LANES_V7X),
                            _pad_cols(gl, LORA_PAD - LORA_G_OFF)], axis=1)


def _ungroup_lora_cols(a):
    return jnp.concatenate([a[..., LORA_W_OFF:LORA_W_OFF + LORA_DECAY], a[..., LORA_A_OFF:LORA_A_OFF + LORA_AAA],
                            a[..., LORA_G_OFF:LORA_G_OFF + LORA_GATE]], axis=-1)


def _layer_weights(l, norm_mix_g, w_in, conv_dw_w, conv_dw_b, conv_ln_g, conv_ln_b, w_conv_out,
                   rw_mu, rw_w0, rw_w2, rw_a0, rw_a2, rw_g2, rw_k_k, rw_k_a, rw_r_k, rw_ln_g, rw_ln_b, w_rw_out,
                   w_mix_out, norm_ffn_g, w_up, ffn_dw_w, ffn_dw_b, w_down):
    row = lambda a: a[l].reshape(1, -1)
    wi = w_in[l]
    o = 2 * D_CONV
    return dict(
        norm_mix_g=row(norm_mix_g),
        wc=wi[:, :o].astype(BF16),
        wrkv=wi[:, o:o + D_RKV].astype(BF16),
        wlora=_regroup_lora_cols(wi[:, o + D_RKV:o + D_RW_IN]).astype(BF16),
        wg=wi[:, o + D_RW_IN:].astype(BF16),
        conv_w=conv_dw_w[l], conv_b=row(conv_dw_b), conv_lg=row(conv_ln_g), conv_lb=row(conv_ln_b),
        wco=w_conv_out[l].astype(BF16),
        murkv=row(rw_mu)[:, :D_RKV], mulora=_regroup_lora_cols(row(rw_mu)[:, D_RKV:]),
        w0=row(rw_w0), w2=_pad_rows(rw_w2[l], LANES_V7X).astype(BF16),
        a0=row(rw_a0), a2=_pad_rows(rw_a2[l], LANES_V7X).astype(BF16),
        g2=_pad_rows(rw_g2[l], LORA_PAD - LORA_G_OFF).astype(BF16),
        kk=row(rw_k_k), ka=row(rw_k_a), rk=row(rw_r_k), lng=row(rw_ln_g), lnb=row(rw_ln_b),
        wro=w_rw_out[l].astype(BF16), wmo=w_mix_out[l].astype(BF16),
        norm_ffn_g=row(norm_ffn_g), wup=w_up[l].astype(BF16),
        ffn_w=ffn_dw_w[l], ffn_b=row(ffn_dw_b), wdown=w_down[l].astype(BF16),
    )


def _layer(x, p, *, nb, stride, chunk, group, conv_n, conv_hist, shift_state, wkv_state, ffn_hist, wkv_layout):
    rows = x.shape[0]
    t = rows // nb
    u, zrkv, zlora, gate = _inproj(x, p["norm_mix_g"], p["wc"], p["wrkv"], p["wlora"], p["wg"])
    g3 = lambda a: a.reshape(nb, t, a.shape[-1])
    c, new_conv = _conv_branch(g3(u), conv_hist, p["conv_w"], p["conv_b"], p["conv_lg"], p["conv_lb"],
                               n=conv_n, stride=stride)
    srkv, slora = shift_state if shift_state is not None else (None, None)
    r, k, v, lw, alr, go = _rwkv_prep(g3(zrkv), g3(zlora), srkv, slora, p["murkv"], p["mulora"],
                                      p["w0"], p["w2"], p["a0"], p["a2"], p["g2"], n=ROW_TILE, stride=stride)
    to_seq, from_seq = wkv_layout
    y, new_wkv = _wkv(*(to_seq(a) for a in (r, k, v, lw, alr, go)), wkv_state,
                      p["kk"], p["ka"], p["rk"], p["lng"], p["lnb"], chunk=chunk, group=group)
    y = from_seq(y)
    x1 = _mix(x, c.reshape(rows, D_CONV), y.reshape(rows, D_MODEL), gate, p["wco"], p["wro"], p["wmo"])
    up = _ffn_up(x1, p["norm_ffn_g"], p["wup"])
    x2 = _ffn_down(g3(up), g3(x1), ffn_hist, p["ffn_w"], p["ffn_b"], p["wdown"], n=ROW_TILE, stride=stride)
    return x2.reshape(rows, D_MODEL), new_conv, zrkv, zlora, new_wkv, up


def kernel(x_prompt, x_sample, state_conv, state_shift, state_wkv, state_ffn,
           norm_mix_g, w_in, conv_dw_w, conv_dw_b, conv_ln_g, conv_ln_b, w_conv_out,
           rw_mu, rw_w0, rw_w2, rw_a0, rw_a2, rw_g2, rw_k_k, rw_k_a, rw_r_k, rw_ln_g, rw_ln_b, w_rw_out,
           w_mix_out, norm_ffn_g, w_up, ffn_dw_w, ffn_dw_b, w_down, norm_final_g):
    params = (norm_mix_g, w_in, conv_dw_w, conv_dw_b, conv_ln_g, conv_ln_b, w_conv_out,
              rw_mu, rw_w0, rw_w2, rw_a0, rw_a2, rw_g2, rw_k_k, rw_k_a, rw_r_k, rw_ln_g, rw_ln_b, w_rw_out,
              w_mix_out, norm_ffn_g, w_up, ffn_dw_w, ffn_dw_b, w_down)
    depth = w_in.shape[0]
    b_p, t_p, _ = x_prompt.shape
    b_s, t_s, _ = x_sample.shape
    final_g = norm_final_g.reshape(1, -1)

    xp = x_prompt.reshape(b_p * t_p, D_MODEL)
    xs = jnp.transpose(x_sample, (1, 0, 2)).reshape(t_s * b_s, D_MODEL)

    ident = lambda a: a
    prompt_layout = (ident, ident)

    def sample_to_seq(a):
        a = jnp.transpose(a.reshape(t_s, b_s, a.shape[-1]), (1, 0, 2))
        return jnp.pad(a, ((0, 0), (0, SAMPLE_CHUNK - t_s), (0, 0)))

    def sample_from_seq(a):
        return jnp.transpose(a[:, :t_s], (1, 0, 2)).reshape(1, t_s * b_s, a.shape[-1])

    p_conv, p_shift, p_wkv, p_ffn = [], [], [], []
    s_conv, s_shift, s_wkv, s_ffn = [], [], [], []
    for l in range(depth):
        p = _layer_weights(l, *params)

        xp, nc, zrkv, zlora, nw, up = _layer(
            xp, p, nb=b_p, stride=1, chunk=PROMPT_CHUNK, group=PROMPT_GROUP, conv_n=2 * ROW_TILE, conv_hist=None, shift_state=None,
            wkv_state=None, ffn_hist=None, wkv_layout=prompt_layout)
        p_conv.append(nc)
        last = lambda a: a.reshape(b_p, t_p, a.shape[-1])[:, t_p - 1]
        p_shift.append(jnp.concatenate([last(zrkv), _ungroup_lora_cols(last(zlora))], axis=-1))
        p_wkv.append(nw)
        p_ffn.append(up.reshape(b_p, t_p, 2 * D_FF)[:, t_p - (FFN_CONV_W - 1):])

        tm = lambda a: jnp.transpose(a, (1, 0, 2)).reshape(1, a.shape[1] * b_s, a.shape[2])
        sh = state_shift[l]
        xs, nc, zrkv, zlora, nw, up = _layer(
            xs, p, nb=1, stride=b_s, chunk=SAMPLE_CHUNK, group=SAMPLE_GROUP, conv_n=t_s * b_s, conv_hist=tm(state_conv[l]),
            shift_state=(sh[:, :D_RKV].reshape(1, b_s, D_RKV),
                         _regroup_lora_cols(sh[:, D_RKV:]).reshape(1, b_s, LORA_PAD)),
            wkv_state=state_wkv[l], ffn_hist=tm(state_ffn[l]), wkv_layout=(sample_to_seq, sample_from_seq))
        s_conv.append(jnp.transpose(nc.reshape(CONV_W - 1, b_s, D_CONV), (1, 0, 2)))
        lastrows = lambda a: a[(t_s - 1) * b_s:]
        s_shift.append(jnp.concatenate([lastrows(zrkv), _ungroup_lora_cols(lastrows(zlora))], axis=-1))
        s_wkv.append(nw)
        s_ffn.append(jnp.transpose(up[(t_s - (FFN_CONV_W - 1)) * b_s:].reshape(FFN_CONV_W - 1, b_s, 2 * D_FF),
                                   (1, 0, 2)))

    y_prompt = _final_norm(xp, final_g).reshape(b_p, t_p, D_MODEL)
    y_sample = jnp.transpose(_final_norm(xs, final_g).reshape(t_s, b_s, D_MODEL), (1, 0, 2))
    return (y_prompt, y_sample, jnp.stack(p_conv), jnp.stack(p_shift), jnp.stack(p_wkv), jnp.stack(p_ffn),
            jnp.stack(s_conv), jnp.stack(s_shift), jnp.stack(s_wkv), jnp.stack(s_ffn))
```

```python
import functools

import jax
import jax.numpy as jnp
from jax import lax
from jax.experimental import pallas as pl
from jax.experimental.pallas import tpu as pltpu

F32 = jnp.float32
BF16 = jnp.bfloat16

D_MODEL = 1024
D_CONV = D_MODEL // 2
CONV_W = 31
HEAD = 64
N_HEADS = D_MODEL // HEAD
LORA_DECAY = 64
LORA_AAA = 64
LORA_GATE = 160
D_FF = 3 * D_MODEL
FFN_CONV_W = 3
D_RKV = 3 * D_MODEL
D_RW_IN = D_RKV + LORA_DECAY + LORA_AAA + LORA_GATE
RMS_EPS = 1e-6
LN_EPS = 1e-5
GN_EPS = 64e-5

SUBLANES_V7X = 8
LANES_V7X = 128
MXU_DIM_V7X = 256
VMEM_LIMIT_BYTES_V7X = 56 * 1024 * 1024

LORA_PAD = 4 * LANES_V7X
LORA_W_OFF, LORA_A_OFF, LORA_G_OFF = 0, LANES_V7X, 2 * LANES_V7X

ROW_TILE = 128
FRONT_TILE = 256
CONV_TILE = 256
CONV_ROW_BLOCK = 64
PROMPT_CHUNK = 64
SAMPLE_CHUNK = 8
PROMPT_GROUP = 2
SAMPLE_GROUP = 4
SAMPLE_SPLIT = 4


def _cparams(n_grid):
    return pltpu.CompilerParams(dimension_semantics=("arbitrary",) * n_grid,
                                vmem_limit_bytes=VMEM_LIMIT_BYTES_V7X)


def _const_spec(shape):
    return pl.BlockSpec(shape, lambda *_: (0,) * len(shape), pipeline_mode=pl.Buffered(1))


def _dot(a, b):
    return jnp.dot(a, b, preferred_element_type=F32)


def _rmsnorm(x, g):
    return x * lax.rsqrt(jnp.mean(x * x, axis=-1, keepdims=True) + RMS_EPS) * g


def _front_kernel(*refs, n, stride, has_state):
    if has_state:
        (x_ref, srkv_ref, slora_ref, g_ref, wc_ref, wrkv_ref, wlora_ref, wg_ref, murkv_ref, mulora_ref,
         w0_ref, w2_ref, a0_ref, a2_ref, g2_ref,
         u_ref, gate_ref, r_ref, k_ref, v_ref, lw_ref, alr_ref, go_ref, nrkv_ref, nlora_ref,
         frkv_scr, flora_scr) = refs
    else:
        (x_ref, g_ref, wc_ref, wrkv_ref, wlora_ref, wg_ref, murkv_ref, mulora_ref,
         w0_ref, w2_ref, a0_ref, a2_ref, g2_ref,
         u_ref, gate_ref, r_ref, k_ref, v_ref, lw_ref, alr_ref, go_ref, nrkv_ref, nlora_ref,
         frkv_scr, flora_scr) = refs
    p = max(SUBLANES_V7X, stride)

    @pl.when(pl.program_id(1) == 0)
    def _():
        if has_state:
            frkv_scr[0:p, :] = srkv_ref[...]
            flora_scr[0:p, :] = slora_ref[...]
        else:
            frkv_scr[0:p, :] = jnp.zeros((p, D_RKV), F32)
            flora_scr[0:p, :] = jnp.zeros((p, LORA_PAD), F32)

    hb = _rmsnorm(x_ref[...], g_ref[...]).astype(BF16)
    zc = _dot(hb, wc_ref[...])
    u_ref[...] = zc[:, :D_CONV] * jax.nn.sigmoid(zc[:, D_CONV:])
    gate_ref[...] = jax.nn.sigmoid(_dot(hb, wg_ref[...]))
    zrkv = _dot(hb, wrkv_ref[...])
    zlora = _dot(hb, wlora_ref[...])

    frkv_scr[p:p + n, :] = zrkv
    flora_scr[p:p + n, :] = zlora
    xs = zrkv + (frkv_scr[p - stride:p - stride + n, :] - zrkv) * murkv_ref[...]
    xl = zlora + (flora_scr[p - stride:p - stride + n, :] - zlora) * mulora_ref[...]
    last_rkv = frkv_scr[n:n + p, :]
    last_lora = flora_scr[n:n + p, :]
    nrkv_ref[...] = last_rkv
    nlora_ref[...] = last_lora
    frkv_scr[0:p, :] = last_rkv
    flora_scr[0:p, :] = last_lora

    r_ref[...] = xs[:, 0:D_MODEL]
    k_ref[...] = xs[:, D_MODEL:2 * D_MODEL]
    v_ref[...] = xs[:, 2 * D_MODEL:3 * D_MODEL]

    wl = jnp.tanh(xl[:, LORA_W_OFF:LORA_W_OFF + LANES_V7X]).astype(BF16)
    al = xl[:, LORA_A_OFF:LORA_A_OFF + LANES_V7X].astype(BF16)
    gl = jax.nn.sigmoid(xl[:, LORA_G_OFF:LORA_PAD]).astype(BF16)
    t = -(w0_ref[...] + _dot(wl, w2_ref[...]))
    softplus = jnp.maximum(t, 0.0) + jnp.log(1.0 + jnp.exp(-jnp.abs(t)))
    lw_ref[...] = -jnp.exp(-softplus - 0.5)
    alr_ref[...] = jax.nn.sigmoid(a0_ref[...] + _dot(al, a2_ref[...]))
    go_ref[...] = _dot(gl, g2_ref[...])


def _front(x, srkv, slora, p, *, n, stride):
    nb, t, _ = x.shape
    has_state = srkv is not None
    pr = max(SUBLANES_V7X, stride)
    tile = lambda c: pl.BlockSpec((None, n, c), lambda i, j: (i, j, 0))
    state = lambda c: pl.BlockSpec((None, pr, c), lambda i, j: (i, 0, 0))
    params = (p["norm_mix_g"], p["wc"], p["wrkv"], p["wlora"], p["wg"], p["murkv"], p["mulora"],
              p["w0"], p["w2"], p["a0"], p["a2"], p["g2"])
    ins = [x] + ([srkv, slora] if has_state else []) + list(params)
    in_specs = ([tile(D_MODEL)] + ([state(D_RKV), state(LORA_PAD)] if has_state else [])
                + [_const_spec(a.shape) for a in params])
    full = lambda c: jax.ShapeDtypeStruct((nb, t, c), F32)
    return pl.pallas_call(
        functools.partial(_front_kernel, n=n, stride=stride, has_state=has_state),
        grid=(nb, t // n),
        in_specs=in_specs,
        out_specs=[tile(D_CONV), tile(2 * D_MODEL)] + [tile(D_MODEL)] * 6 + [state(D_RKV), state(LORA_PAD)],
        out_shape=[full(D_CONV), full(2 * D_MODEL)] + [full(D_MODEL)] * 6
                  + [jax.ShapeDtypeStruct((nb, pr, D_RKV), F32), jax.ShapeDtypeStruct((nb, pr, LORA_PAD), F32)],
        scratch_shapes=[pltpu.VMEM((pr + n, D_RKV), F32), pltpu.VMEM((pr + n, LORA_PAD), F32)],
        compiler_params=_cparams(2),
        name="front",
    )(*ins)


def _conv_kernel(*refs, n, n_steps, stride, has_hist):
    refs = list(refs)
    u_ref = refs.pop(0)
    hist_ref = refs.pop(0) if has_hist else None
    w_ref, b_ref, lg_ref, lb_ref, c_ref, newhist_ref, full_scr = refs[:7]
    aligned = stride % SUBLANES_V7X == 0
    shift_scr = None if aligned else refs[7]
    hist_rows = (CONV_W - 1) * stride
    pad = (-hist_rows) % SUBLANES_V7X
    base = pad + hist_rows

    @pl.when(pl.program_id(1) == 0)
    def _():
        if has_hist:
            if pad:
                full_scr[0:pad, :] = jnp.zeros((pad, D_CONV), F32)
            full_scr[pad:base, :] = hist_ref[...]
        else:
            full_scr[0:base, :] = jnp.zeros((base, D_CONV), F32)

    full_scr[base:base + n, :] = u_ref[...]
    if not aligned:
        span = base + n - SUBLANES_V7X
        for s in range(1, SUBLANES_V7X):
            shift_scr[s - 1, 0:span, :] = full_scr[s:s + span, :]

    def tap(j, r0):
        o = pad + j * stride
        s = o % SUBLANES_V7X
        a = r0 + o - s
        if s == 0:
            return full_scr[a:a + CONV_ROW_BLOCK, :]
        return shift_scr[s - 1, a:a + CONV_ROW_BLOCK, :]

    w = w_ref[...]
    for rb in range(n // CONV_ROW_BLOCK):
        r0 = rb * CONV_ROW_BLOCK
        acc = jnp.broadcast_to(b_ref[...], (CONV_ROW_BLOCK, D_CONV))
        for j in range(CONV_W):
            acc = acc + w[j:j + 1, :] * tap(j, r0)
        mu = jnp.mean(acc, axis=-1, keepdims=True)
        xc = acc - mu
        var = jnp.mean(xc * xc, axis=-1, keepdims=True)
        y = xc * lax.rsqrt(var + LN_EPS) * lg_ref[...] + lb_ref[...]
        c_ref[r0:r0 + CONV_ROW_BLOCK, :] = (y * jax.nn.sigmoid(y)).astype(BF16)

    newhist_ref[...] = full_scr[pad + n:base + n, :]
    if n_steps > 1:
        full_scr[0:base, :] = full_scr[n:n + base, :]


def _conv_branch(u, hist, w, b, lg, lb, *, n, stride):
    nb, t, _ = u.shape
    has_hist = hist is not None
    hist_rows = (CONV_W - 1) * stride
    base = hist_rows + (-hist_rows) % SUBLANES_V7X
    tile = pl.BlockSpec((None, n, D_CONV), lambda i, j: (i, j, 0))
    hspec = pl.BlockSpec((None, hist_rows, D_CONV), lambda i, j: (i, 0, 0))
    ins = [u] + ([hist] if has_hist else []) + [w, b, lg, lb]
    in_specs = [tile] + ([hspec] if has_hist else []) + [_const_spec(a.shape) for a in (w, b, lg, lb)]
    scratch = [pltpu.VMEM((base + n, D_CONV), F32)]
    if stride % SUBLANES_V7X:
        scratch.append(pltpu.VMEM((SUBLANES_V7X - 1, base + n, D_CONV), F32))
    return pl.pallas_call(
        functools.partial(_conv_kernel, n=n, n_steps=t // n, stride=stride, has_hist=has_hist),
        grid=(nb, t // n),
        in_specs=in_specs,
        out_specs=[tile, hspec],
        out_shape=[jax.ShapeDtypeStruct((nb, t, D_CONV), BF16), jax.ShapeDtypeStruct((nb, hist_rows, D_CONV), F32)],
        scratch_shapes=scratch,
        compiler_params=_cparams(2),
        name="conv_branch",
    )(*ins)


def _wkv_kernel(*refs, chunk, n_chunks, group, has_state):
    if has_state:
        (r_ref, k_ref, v_ref, lw_ref, alr_ref, go_ref, s0_ref, _, kk_ref, ka_ref, rk_ref, lng_ref, lnb_ref,
         y_ref, sout_ref, s_scr) = refs
    else:
        (r_ref, k_ref, v_ref, lw_ref, alr_ref, go_ref, kk_ref, ka_ref, rk_ref, lng_ref, lnb_ref,
         y_ref, sout_ref, s_scr) = refs
    c = chunk
    transposed_state = not has_state

    @pl.when(pl.program_id(1) == 0)
    def _():
        if has_state:
            s_scr[...] = s0_ref[...]
        else:
            s_scr[...] = jnp.zeros_like(s_scr)

    row = lax.broadcasted_iota(jnp.int32, (c, c), 0)
    col = lax.broadcasted_iota(jnp.int32, (c, c), 1)
    incl = row >= col
    strict = row > col
    tri = jnp.where(incl, 1.0, 0.0).astype(BF16)
    sh = lambda x, s: lax.shift_right_logical(x, jnp.int32(s))
    eye = row == col
    first = strict & (sh(row, 1) == sh(col, 1))
    offs = []
    lg = 1
    while (1 << lg) < c:
        offs.append((1 << lg, (sh(row, lg + 1) == sh(col, lg + 1))
                     & ((sh(row, lg) & 1) == 1) & ((sh(col, lg) & 1) == 0)))
        lg += 1

    srow = lax.broadcasted_iota(jnp.int32, (MXU_DIM_V7X, MXU_DIM_V7X), 0)
    scol = lax.broadcasted_iota(jnp.int32, (MXU_DIM_V7X, MXU_DIM_V7X), 1)
    head_lg = HEAD.bit_length() - 1
    blockdiag = jnp.where(sh(srow, head_lg) == sh(scol, head_lg), 1.0, 0.0).astype(BF16)

    def head_sums(xs):
        xb = [x.astype(BF16) for x in xs]
        return [jnp.concatenate([_dot(b[:, q:q + MXU_DIM_V7X], blockdiag) for q in range(0, D_MODEL, MXU_DIM_V7X)],
                                axis=1) for b in xb]

    gs = range(group)
    r = [r_ref[g] for g in gs]
    k = [k_ref[g] for g in gs]
    v = [v_ref[g] for g in gs]
    lw = [lw_ref[g] for g in gs]
    alr = [alr_ref[g] for g in gs]
    kkraw = [k[g] * kk_ref[...] for g in gs]
    kmod = [k[g] * (1.0 + (alr[g] - 1.0) * ka_ref[...]) for g in gs]
    hi = [lw[g].astype(BF16) for g in gs]
    rem = [lw[g] - hi[g].astype(F32) for g in gs]
    mid = [rem[g].astype(BF16) for g in gs]
    lo = [(rem[g] - mid[g].astype(F32)).astype(BF16) for g in gs]
    cum = [_dot(tri, hi[g]) + _dot(tri, mid[g]) + _dot(tri, lo[g]) for g in gs]
    nrm2 = head_sums([kkraw[g] * kkraw[g] for g in gs])
    w_incl = [jnp.exp(cum[g]) for g in gs]
    w_excl = [jnp.exp(cum[g] - lw[g]) for g in gs]
    w_inv = [jnp.exp(-cum[g]) for g in gs]
    w_last = [w_incl[g][c - 1:c, :] for g in gs]
    kk = [kkraw[g] / jnp.maximum(jnp.sqrt(nrm2[g]), 1e-12) for g in gs]
    a_t = [-kk[g] * w_excl[g] for g in gs]
    b_t = [kk[g] * alr[g] * w_inv[g] for g in gs]
    r_t = [r[g] * w_incl[g] for g in gs]
    k_t = [kmod[g] * w_inv[g] for g in gs]
    lhs_f = [jnp.concatenate([a_t[g], r_t[g]], axis=0).astype(BF16) for g in gs]
    rhs_f = [jnp.concatenate([b_t[g], k_t[g]], axis=0) for g in gs]
    rhs_b = [rhs_f[g].astype(BF16) for g in gs]
    rhs_w = [rhs_f[g] * w_last[g] for g in gs]
    v_b = [v[g].astype(BF16) for g in gs]
    if transposed_state:
        rhs_wt = [[rhs_w[g][:, q:q + LANES_V7X].T.astype(BF16) for q in range(0, D_MODEL, LANES_V7X)] for g in gs]
        decay = [[jnp.broadcast_to(w_last[g][:, q:q + LANES_V7X], (LANES_V7X, LANES_V7X)).T
                  for q in range(0, D_MODEL, LANES_V7X)] for g in gs]
    else:
        rhs_wb = [rhs_w[g].astype(BF16) for g in gs]

    chains = [(g, h) for g in gs for h in range(N_HEADS)]
    sl = lambda h: slice(h * HEAD, (h + 1) * HEAD)
    nt = (((1,), (1,)), ((), ()))
    tn = (((0,), (0,)), ((), ()))
    s_old = [s_scr[g, h] for g, h in chains]
    pm = [lax.dot_general(lhs_f[g][:, sl(h)], rhs_b[g][:, sl(h)], nt, preferred_element_type=F32)
          for g, h in chains]
    if transposed_state:
        gm = [_dot(lhs_f[g][:, sl(h)], s_old[i].astype(BF16)) for i, (g, h) in enumerate(chains)]
    else:
        gm = [lax.dot_general(lhs_f[g][:, sl(h)], s_old[i].astype(BF16), nt, preferred_element_type=F32)
              for i, (g, h) in enumerate(chains)]
    idx = range(len(chains))
    low = [jnp.where(strict, pm[i][:c, :c], 0.0) for i in idx]
    pakv = [_dot(jnp.where(strict, pm[i][:c, c:], 0.0).astype(BF16), v_b[g][:, sl(h)])
            for i, (g, h) in enumerate(chains)]

    def odd_rows(x, s):
        return jnp.concatenate([x[b * s:(b + 1) * s] for b in range(1, c // s, 2)], axis=0)

    def with_odd_rows(x, odd, s):
        return jnp.concatenate([odd[(b // 2) * s:(b // 2 + 1) * s] if b % 2 else x[b * s:(b + 1) * s]
                                for b in range(c // s)], axis=0)

    t = [jnp.where(eye, 1.0, 0.0) + jnp.where(first, low[i], 0.0) for i in idx]
    for s, m in offs:
        tb = [t[i].astype(BF16) for i in idx]
        if s % SUBLANES_V7X:
            a = [_dot(jnp.where(m, low[i], 0.0).astype(BF16), tb[i]) for i in idx]
            t = [t[i] + _dot(tb[i], a[i].astype(BF16)) for i in idx]
        else:
            zero = jnp.zeros((c, c), F32)
            a = [_dot(odd_rows(jnp.where(m, low[i], 0.0), s).astype(BF16), tb[i]) for i in idx]
            t_odd = [odd_rows(t[i], s) for i in idx]
            upd = [_dot(t_odd[i].astype(BF16), with_odd_rows(zero, a[i], s).astype(BF16)) for i in idx]
            t = [with_odd_rows(t[i], t_odd[i] + upd[i], s) for i in idx]
    u = [_dot(t[i].astype(BF16), (gm[i][:c] + pakv[i]).astype(BF16)) for i in idx]
    uv = [jnp.concatenate([u[i].astype(BF16), v_b[g][:, sl(h)]], axis=0) for i, (g, h) in enumerate(chains)]
    m2 = [jnp.concatenate([jnp.where(incl, pm[i][c:, :c], 0.0), jnp.where(incl, pm[i][c:, c:], 0.0)],
                          axis=1).astype(BF16) for i in idx]
    y = [gm[i][c:] + _dot(m2[i], uv[i]) for i in idx]
    for i, (g, h) in enumerate(chains):
        if transposed_state:
            q, half = divmod(h, 2)
            rows = slice(half * HEAD, (half + 1) * HEAD)
            s_scr[g, h] = s_old[i] * decay[g][q][rows, :HEAD] + _dot(rhs_wt[g][q][rows, :], uv[i])
        else:
            s_scr[g, h] = (s_old[i] * w_last[g][:, sl(h)]
                           + lax.dot_general(uv[i], rhs_wb[g][:, sl(h)], tn, preferred_element_type=F32))

    y_f = [jnp.concatenate(y[g * N_HEADS:(g + 1) * N_HEADS], axis=1) for g in gs]
    mu = head_sums(y_f)
    yc = [y_f[g] - mu[g] * (1.0 / HEAD) for g in gs]
    var_bonus = head_sums([yc[g] * yc[g] for g in gs] + [r[g] * kmod[g] * rk_ref[...] for g in gs])
    for g in gs:
        yn = yc[g] * lax.rsqrt(var_bonus[g] * (1.0 / HEAD) + GN_EPS) * lng_ref[...] + lnb_ref[...]
        y_ref[g] = (yn + var_bonus[group + g] * v[g]) * go_ref[g]

    @pl.when(pl.program_id(1) == n_chunks - 1)
    def _():
        if transposed_state:
            hrow = lax.broadcasted_iota(jnp.int32, (HEAD, HEAD), 0)
            hcol = lax.broadcasted_iota(jnp.int32, (HEAD, HEAD), 1)
            ident = jnp.where(hrow == hcol, 1.0, 0.0).astype(BF16)
            xt = lambda p: lax.dot_general(p, ident, tn, preferred_element_type=F32)
            for g, h in chains:
                st = s_scr[g, h]
                p0 = st.astype(BF16)
                rem0 = st - p0.astype(F32)
                p1 = rem0.astype(BF16)
                p2 = (rem0 - p1.astype(F32)).astype(BF16)
                sout_ref[g, h] = xt(p0) + xt(p1) + xt(p2)
        else:
            sout_ref[...] = s_scr[...]


def _wkv(r, k, v, lw, alr, go, state, p, *, chunk, group):
    nb, t, _ = r.shape
    has_state = state is not None
    n_chunks = t // chunk
    tile = pl.BlockSpec((group, chunk, D_MODEL), lambda i, j: (i, j, 0))
    params = (p["kk"], p["ka"], p["rk"], p["lng"], p["lnb"])
    pspecs = [_const_spec(a.shape) for a in params]
    kern = functools.partial(_wkv_kernel, chunk=chunk, n_chunks=n_chunks, group=group, has_state=has_state)
    y_shape = jax.ShapeDtypeStruct((nb, t, D_MODEL), F32)
    scratch = [pltpu.VMEM((group, N_HEADS, HEAD, HEAD), F32)]
    if not has_state:
        sspec = pl.BlockSpec((group, N_HEADS, HEAD, HEAD), lambda i, j: (i, 0, 0, 0))
        return pl.pallas_call(
            kern, grid=(nb // group, n_chunks),
            in_specs=[tile] * 6 + pspecs,
            out_specs=[tile, sspec],
            out_shape=[y_shape, jax.ShapeDtypeStruct((nb, N_HEADS, HEAD, HEAD), F32)],
            scratch_shapes=scratch, compiler_params=_cparams(2), name="wkv",
        )(r, k, v, lw, alr, go, *params)
    all_states, out_states, layer = state
    lspec = pl.BlockSpec((None, group, N_HEADS, HEAD, HEAD), lambda i, j: (layer, i, 0, 0, 0))
    return pl.pallas_call(
        kern, grid=(nb // group, n_chunks),
        in_specs=[tile] * 6 + [lspec, pl.BlockSpec(memory_space=pl.ANY)] + pspecs,
        out_specs=[tile, lspec],
        out_shape=[y_shape, jax.ShapeDtypeStruct(out_states.shape, F32)],
        input_output_aliases={7: 1},
        scratch_shapes=scratch, compiler_params=_cparams(2), name="wkv",
    )(r, k, v, lw, alr, go, all_states, out_states, *params)


GELU_C0 = 0.7978845608028654
GELU_C1 = 0.044715


def _back_kernel(*refs, n, stride, has_hist, final_norm):
    refs = list(refs)
    x_ref, c_ref, y_ref, gate_ref = refs[:4]
    del refs[:4]
    hist_ref = refs.pop(0) if has_hist else None
    wco_ref, wro_ref, wmo_ref, gffn_ref, wup_ref, fw_ref, fb_ref, wd_ref = refs[:8]
    del refs[:8]
    gfin_ref = refs.pop(0) if final_norm else None
    o_ref, newhist_ref, full_scr = refs
    hist_rows = (FFN_CONV_W - 1) * stride
    p = max(SUBLANES_V7X, hist_rows)

    @pl.when(pl.program_id(1) == 0)
    def _():
        if has_hist:
            full_scr[0:p, :] = hist_ref[...]
        else:
            full_scr[0:p, :] = jnp.zeros((p, 2 * D_FF), F32)

    ya = _dot(c_ref[...], wco_ref[...])
    yb = _dot(y_ref[...].astype(BF16), wro_ref[...])
    g = gate_ref[...]
    m = g[:, :D_MODEL] * ya + g[:, D_MODEL:] * yb
    x1 = x_ref[...] + _dot(m.astype(BF16), wmo_ref[...])

    up = _dot(_rmsnorm(x1, gffn_ref[...]).astype(BF16), wup_ref[...])
    full_scr[p:p + n, :] = up
    w = fw_ref[...]
    cu = (fb_ref[...] + w[0:1, :] * full_scr[p - 2 * stride:p - 2 * stride + n, :]
          + w[1:2, :] * full_scr[p - stride:p - stride + n, :] + w[2:3, :] * up)
    last = full_scr[n:n + p, :]
    newhist_ref[...] = last
    full_scr[0:p, :] = last
    a = cu[:, :D_FF]
    gelu = 0.5 * a * (1.0 + jnp.tanh(GELU_C0 * (a + GELU_C1 * (a * a * a))))
    f = (gelu * cu[:, D_FF:]).astype(BF16)
    x2 = x1 + _dot(f, wd_ref[...])
    o_ref[...] = _rmsnorm(x2, gfin_ref[...]) if final_norm else x2


def _back(x, c, y, gate, hist, p, final_g, *, n, stride):
    nb, t, _ = x.shape
    has_hist = hist is not None
    final_norm = final_g is not None
    pr = max(SUBLANES_V7X, (FFN_CONV_W - 1) * stride)
    tile = lambda w: pl.BlockSpec((None, n, w), lambda i, j: (i, j, 0))
    hspec = pl.BlockSpec((None, pr, 2 * D_FF), lambda i, j: (i, 0, 0))
    params = [p["wco"], p["wro"], p["wmo"], p["norm_ffn_g"], p["wup"], p["ffn_w"], p["ffn_b"], p["wdown"]]
    if final_norm:
        params.append(final_g)
    ins = [x, c, y, gate] + ([hist] if has_hist else []) + params
    in_specs = ([tile(D_MODEL), tile(D_CONV), tile(D_MODEL), tile(2 * D_MODEL)] + ([hspec] if has_hist else [])
                + [_const_spec(a.shape) for a in params])
    return pl.pallas_call(
        functools.partial(_back_kernel, n=n, stride=stride, has_hist=has_hist, final_norm=final_norm),
        grid=(nb, t // n),
        in_specs=in_specs,
        out_specs=[tile(D_MODEL), hspec],
        out_shape=[jax.ShapeDtypeStruct((nb, t, D_MODEL), F32), jax.ShapeDtypeStruct((nb, pr, 2 * D_FF), F32)],
        scratch_shapes=[pltpu.VMEM((pr + n, 2 * D_FF), F32)],
        compiler_params=_cparams(2),
        name="back",
    )(*ins)


def _pad_cols(a, width):
    return jnp.pad(a, ((0, 0), (0, width - a.shape[1])))


def _pad_rows(a, height):
    return jnp.pad(a, ((0, height - a.shape[0]), (0, 0)))


def _regroup_lora_cols(a):
    wl = a[:, 0:LORA_DECAY]
    al = a[:, LORA_DECAY:LORA_DECAY + LORA_AAA]
    gl = a[:, LORA_DECAY + LORA_AAA:]
    return jnp.concatenate([_pad_cols(wl, LANES_V7X), _pad_cols(al, LANES_V7X),
                            _pad_cols(gl, LORA_PAD - LORA_G_OFF)], axis=1)


def _ungroup_lora_cols(a):
    return jnp.concatenate([a[..., LORA_W_OFF:LORA_W_OFF + LORA_DECAY], a[..., LORA_A_OFF:LORA_A_OFF + LORA_AAA],
                            a[..., LORA_G_OFF:LORA_G_OFF + LORA_GATE]], axis=-1)


def _layer_weights(l, norm_mix_g, w_in, conv_dw_w, conv_dw_b, conv_ln_g, conv_ln_b, w_conv_out,
                   rw_mu, rw_w0, rw_w2, rw_a0, rw_a2, rw_g2, rw_k_k, rw_k_a, rw_r_k, rw_ln_g, rw_ln_b, w_rw_out,
                   w_mix_out, norm_ffn_g, w_up, ffn_dw_w, ffn_dw_b, w_down):
    row = lambda a: a[l].reshape(1, -1)
    wi = w_in[l]
    o = 2 * D_CONV
    return dict(
        norm_mix_g=row(norm_mix_g),
        wc=wi[:, :o].astype(BF16),
        wrkv=wi[:, o:o + D_RKV].astype(BF16),
        wlora=_regroup_lora_cols(wi[:, o + D_RKV:o + D_RW_IN]).astype(BF16),
        wg=wi[:, o + D_RW_IN:].astype(BF16),
        conv_w=conv_dw_w[l], conv_b=row(conv_dw_b), conv_lg=row(conv_ln_g), conv_lb=row(conv_ln_b),
        wco=w_conv_out[l].astype(BF16),
        murkv=row(rw_mu)[:, :D_RKV], mulora=_regroup_lora_cols(row(rw_mu)[:, D_RKV:]),
        w0=row(rw_w0), w2=_pad_rows(rw_w2[l], LANES_V7X).astype(BF16),
        a0=row(rw_a0), a2=_pad_rows(rw_a2[l], LANES_V7X).astype(BF16),
        g2=_pad_rows(rw_g2[l], LORA_PAD - LORA_G_OFF).astype(BF16),
        kk=row(rw_k_k), ka=row(rw_k_a), rk=row(rw_r_k), lng=row(rw_ln_g), lnb=row(rw_ln_b),
        wro=w_rw_out[l].astype(BF16), wmo=w_mix_out[l].astype(BF16),
        norm_ffn_g=row(norm_ffn_g), wup=w_up[l].astype(BF16),
        ffn_w=ffn_dw_w[l], ffn_b=row(ffn_dw_b), wdown=w_down[l].astype(BF16),
    )


def _layer(x, p, final_g, *, stride, chunk, group, conv_n, conv_hist, shift_state, wkv_state, ffn_hist, wkv_layout):
    srkv, slora = shift_state if shift_state is not None else (None, None)
    u, gate, r, k, v, lw, alr, go, nrkv, nlora = _front(x, srkv, slora, p, n=min(FRONT_TILE, x.shape[1]),
                                                       stride=stride)
    c, new_conv = _conv_branch(u, conv_hist, p["conv_w"], p["conv_b"], p["conv_lg"], p["conv_lb"],
                               n=conv_n, stride=stride)
    to_seq, from_seq = wkv_layout
    y, new_wkv = _wkv(*(to_seq(a) for a in (r, k, v, lw, alr, go)), wkv_state, p, chunk=chunk, group=group)
    x2, new_ffn = _back(x, c, from_seq(y), gate, ffn_hist, p, final_g, n=ROW_TILE, stride=stride)
    return x2, new_conv, nrkv, nlora, new_wkv, new_ffn


def kernel(x_prompt, x_sample, state_conv, state_shift, state_wkv, state_ffn,
           norm_mix_g, w_in, conv_dw_w, conv_dw_b, conv_ln_g, conv_ln_b, w_conv_out,
           rw_mu, rw_w0, rw_w2, rw_a0, rw_a2, rw_g2, rw_k_k, rw_k_a, rw_r_k, rw_ln_g, rw_ln_b, w_rw_out,
           w_mix_out, norm_ffn_g, w_up, ffn_dw_w, ffn_dw_b, w_down, norm_final_g):
    params = (norm_mix_g, w_in, conv_dw_w, conv_dw_b, conv_ln_g, conv_ln_b, w_conv_out,
              rw_mu, rw_w0, rw_w2, rw_a0, rw_a2, rw_g2, rw_k_k, rw_k_a, rw_r_k, rw_ln_g, rw_ln_b, w_rw_out,
              w_mix_out, norm_ffn_g, w_up, ffn_dw_w, ffn_dw_b, w_down)
    depth = w_in.shape[0]
    b_p, t_p, _ = x_prompt.shape
    b_s, t_s, _ = x_sample.shape
    final_g = norm_final_g.reshape(1, -1)

    ns = SAMPLE_SPLIT
    stride = b_s // ns

    def tm(a):
        a = jnp.transpose(a.reshape(ns, stride, a.shape[1], a.shape[2]), (0, 2, 1, 3))
        return a.reshape(ns, a.shape[1] * stride, a.shape[3])

    def untm(a, rows):
        a = jnp.transpose(a.reshape(ns, rows, stride, a.shape[-1]), (0, 2, 1, 3))
        return a.reshape(b_s, rows, a.shape[-1])

    xp = x_prompt
    xs = tm(x_sample)

    ident = lambda a: a
    prompt_layout = (ident, ident)
    sample_to_seq = lambda a: jnp.pad(untm(a, t_s), ((0, 0), (0, SAMPLE_CHUNK - t_s), (0, 0)))
    sample_from_seq = lambda a: tm(a[:, :t_s])

    p_conv, p_shift, p_wkv, p_ffn = [], [], [], []
    s_conv, s_shift, s_ffn = [], [], []
    s_wkv = jnp.zeros(state_wkv.shape, F32)
    for l in range(depth):
        p = _layer_weights(l, *params)
        fin = final_g if l == depth - 1 else None

        xp, nc, nrkv, nlora, nw, nf = _layer(
            xp, p, fin, stride=1, chunk=PROMPT_CHUNK, group=PROMPT_GROUP, conv_n=CONV_TILE, conv_hist=None,
            shift_state=None, wkv_state=None, ffn_hist=None, wkv_layout=prompt_layout)
        p_conv.append(nc)
        p_shift.append(jnp.concatenate([nrkv[:, -1], _ungroup_lora_cols(nlora[:, -1])], axis=-1))
        p_wkv.append(nw)
        p_ffn.append(nf[:, nf.shape[1] - (FFN_CONV_W - 1):])

        sh = state_shift[l]
        xs, nc, nrkv, nlora, s_wkv, nf = _layer(
            xs, p, fin, stride=stride, chunk=SAMPLE_CHUNK, group=SAMPLE_GROUP, conv_n=t_s * stride,
            conv_hist=tm(state_conv[l]),
            shift_state=(sh[:, :D_RKV].reshape(ns, stride, D_RKV),
                         _regroup_lora_cols(sh[:, D_RKV:]).reshape(ns, stride, LORA_PAD)),
            wkv_state=(state_wkv, s_wkv, l), ffn_hist=tm(state_ffn[l]),
            wkv_layout=(sample_to_seq, sample_from_seq))
        s_conv.append(untm(nc, CONV_W - 1))
        s_shift.append(jnp.concatenate([nrkv.reshape(b_s, D_RKV), _ungroup_lora_cols(nlora.reshape(b_s, LORA_PAD))],
                                       axis=-1))
        s_ffn.append(untm(nf, FFN_CONV_W - 1))

    y_prompt = xp
    y_sample = untm(xs, t_s)
    return (y_prompt, y_sample, jnp.stack(p_conv), jnp.stack(p_shift), jnp.stack(p_wkv), jnp.stack(p_ffn),
            jnp.stack(s_conv), jnp.stack(s_shift), s_wkv, jnp.stack(s_ffn))
```

```python
import functools

import jax
import jax.numpy as jnp
from jax import lax
from jax.experimental import pallas as pl
from jax.experimental.pallas import tpu as pltpu

F32 = jnp.float32
BF16 = jnp.bfloat16

D_MODEL = 1024
D_CONV = D_MODEL // 2
CONV_W = 31
HEAD = 64
N_HEADS = D_MODEL // HEAD
LORA_DECAY = 64
LORA_AAA = 64
LORA_GATE = 160
D_FF = 3 * D_MODEL
FFN_CONV_W = 3
D_RKV = 3 * D_MODEL
D_RW_IN = D_RKV + LORA_DECAY + LORA_AAA + LORA_GATE
RMS_EPS = 1e-6
LN_EPS = 1e-5
GN_EPS = 64e-5

SUBLANES_V7X = 8
LANES_V7X = 128
MXU_DIM_V7X = 256
VMEM_LIMIT_BYTES_V7X = 56 * 1024 * 1024

LORA_PAD = 4 * LANES_V7X
LORA_W_OFF, LORA_A_OFF, LORA_G_OFF = 0, LANES_V7X, 2 * LANES_V7X

BACK_TILE = 256
FFN_COL_BLOCK = 256
FRONT_TILE = 256
CONV_TILE = 256
CONV_ROW_BLOCK = 64
PROMPT_CHUNK = 64
SAMPLE_CHUNK = 8
PROMPT_GROUP = 4
SAMPLE_GROUP = 4
SAMPLE_SPLIT = 4


def _cparams(n_grid):
    return pltpu.CompilerParams(dimension_semantics=("arbitrary",) * n_grid,
                                vmem_limit_bytes=VMEM_LIMIT_BYTES_V7X)


def _const_spec(shape):
    return pl.BlockSpec(shape, lambda *_: (0,) * len(shape), pipeline_mode=pl.Buffered(1))


def _layer_spec(a, layer):
    return pl.BlockSpec((None,) + a.shape[1:], lambda *_: (layer,) + (0,) * (a.ndim - 1),
                        pipeline_mode=pl.Buffered(1))


def _dot(a, b):
    return jnp.dot(a, b, preferred_element_type=F32)


def _rmsnorm(x, g):
    return x * lax.rsqrt(jnp.mean(x * x, axis=-1, keepdims=True) + RMS_EPS) * g


def _front_kernel(*refs, n, stride, has_state):
    if has_state:
        (x_ref, srkv_ref, slora_ref, g_ref, wc_ref, wrkv_ref, wlora_ref, wg_ref, murkv_ref, mulora_ref,
         w0_ref, w2_ref, a0_ref, a2_ref, g2_ref,
         u_ref, gate_ref, r_ref, k_ref, v_ref, lw_ref, alr_ref, go_ref, nrkv_ref, nlora_ref,
         frkv_scr, flora_scr) = refs
    else:
        (x_ref, g_ref, wc_ref, wrkv_ref, wlora_ref, wg_ref, murkv_ref, mulora_ref,
         w0_ref, w2_ref, a0_ref, a2_ref, g2_ref,
         u_ref, gate_ref, r_ref, k_ref, v_ref, lw_ref, alr_ref, go_ref, nrkv_ref, nlora_ref,
         frkv_scr, flora_scr) = refs
    p = max(SUBLANES_V7X, stride)

    @pl.when(pl.program_id(1) == 0)
    def _():
        if has_state:
            frkv_scr[0:p, :] = srkv_ref[...]
            flora_scr[0:p, :] = slora_ref[...]
        else:
            frkv_scr[0:p, :] = jnp.zeros((p, D_RKV), F32)
            flora_scr[0:p, :] = jnp.zeros((p, LORA_PAD), F32)

    hb = _rmsnorm(x_ref[...], g_ref[...]).astype(BF16)
    zc = _dot(hb, wc_ref[...])
    u_ref[...] = zc[:, :D_CONV] * jax.nn.sigmoid(zc[:, D_CONV:])
    gate_ref[...] = jax.nn.sigmoid(_dot(hb, wg_ref[...]))
    zrkv = _dot(hb, wrkv_ref[...])
    zlora = _dot(hb, wlora_ref[...])

    frkv_scr[p:p + n, :] = zrkv
    flora_scr[p:p + n, :] = zlora
    xs = zrkv + (frkv_scr[p - stride:p - stride + n, :] - zrkv) * murkv_ref[...]
    xl = zlora + (flora_scr[p - stride:p - stride + n, :] - zlora) * mulora_ref[...]
    last_rkv = frkv_scr[n:n + p, :]
    last_lora = flora_scr[n:n + p, :]
    nrkv_ref[...] = last_rkv
    nlora_ref[...] = last_lora
    frkv_scr[0:p, :] = last_rkv
    flora_scr[0:p, :] = last_lora

    r_ref[...] = xs[:, 0:D_MODEL]
    k_ref[...] = xs[:, D_MODEL:2 * D_MODEL]
    v_ref[...] = xs[:, 2 * D_MODEL:3 * D_MODEL]

    wl = jnp.tanh(xl[:, LORA_W_OFF:LORA_W_OFF + LANES_V7X]).astype(BF16)
    al = xl[:, LORA_A_OFF:LORA_A_OFF + LANES_V7X].astype(BF16)
    gl = jax.nn.sigmoid(xl[:, LORA_G_OFF:LORA_PAD]).astype(BF16)
    t = -(w0_ref[...] + _dot(wl, w2_ref[...]))
    softplus = jnp.maximum(t, 0.0) + jnp.log(1.0 + jnp.exp(-jnp.abs(t)))
    lw_ref[...] = -jnp.exp(-softplus - 0.5)
    alr_ref[...] = jax.nn.sigmoid(a0_ref[...] + _dot(al, a2_ref[...]))
    go_ref[...] = _dot(gl, g2_ref[...])


def _front(x, srkv, slora, p, *, n, stride):
    nb, t, _ = x.shape
    has_state = srkv is not None
    pr = max(SUBLANES_V7X, stride)
    tile = lambda c: pl.BlockSpec((None, n, c), lambda i, j: (i, j, 0))
    state = lambda c: pl.BlockSpec((None, pr, c), lambda i, j: (i, 0, 0))
    params = (p["norm_mix_g"], p["wc"], p["wrkv"], p["wlora"], p["wg"], p["murkv"], p["mulora"],
              p["w0"], p["w2"], p["a0"], p["a2"], p["g2"])
    ins = [x] + ([srkv, slora] if has_state else []) + list(params)
    in_specs = ([tile(D_MODEL)] + ([state(D_RKV), state(LORA_PAD)] if has_state else [])
                + [_layer_spec(a, p["layer"]) for a in params])
    full = lambda c: jax.ShapeDtypeStruct((nb, t, c), F32)
    return pl.pallas_call(
        functools.partial(_front_kernel, n=n, stride=stride, has_state=has_state),
        grid=(nb, t // n),
        in_specs=in_specs,
        out_specs=[tile(D_CONV), tile(2 * D_MODEL)] + [tile(D_MODEL)] * 6 + [state(D_RKV), state(LORA_PAD)],
        out_shape=[full(D_CONV), full(2 * D_MODEL)] + [full(D_MODEL)] * 6
                  + [jax.ShapeDtypeStruct((nb, pr, D_RKV), F32), jax.ShapeDtypeStruct((nb, pr, LORA_PAD), F32)],
        scratch_shapes=[pltpu.VMEM((pr + n, D_RKV), F32), pltpu.VMEM((pr + n, LORA_PAD), F32)],
        compiler_params=_cparams(2),
        name="front",
    )(*ins)


def _conv_kernel(*refs, n, n_steps, stride, has_hist):
    refs = list(refs)
    u_ref = refs.pop(0)
    hist_ref = refs.pop(0) if has_hist else None
    w_ref, b_ref, lg_ref, lb_ref, c_ref, newhist_ref, full_scr = refs[:7]
    aligned = stride % SUBLANES_V7X == 0
    shift_scr = None if aligned else refs[7]
    hist_rows = (CONV_W - 1) * stride
    pad = (-hist_rows) % SUBLANES_V7X
    base = pad + hist_rows

    @pl.when(pl.program_id(1) == 0)
    def _():
        if has_hist:
            if pad:
                full_scr[0:pad, :] = jnp.zeros((pad, D_CONV), F32)
            full_scr[pad:base, :] = hist_ref[...]
        else:
            full_scr[0:base, :] = jnp.zeros((base, D_CONV), F32)

    full_scr[base:base + n, :] = u_ref[...]
    if not aligned:
        span = base + n - SUBLANES_V7X
        for s in range(1, SUBLANES_V7X):
            shift_scr[s - 1, 0:span, :] = full_scr[s:s + span, :]

    def tap(j, r0):
        o = pad + j * stride
        s = o % SUBLANES_V7X
        a = r0 + o - s
        if s == 0:
            return full_scr[a:a + CONV_ROW_BLOCK, :]
        return shift_scr[s - 1, a:a + CONV_ROW_BLOCK, :]

    w = w_ref[...]
    for rb in range(n // CONV_ROW_BLOCK):
        r0 = rb * CONV_ROW_BLOCK
        acc = jnp.broadcast_to(b_ref[...], (CONV_ROW_BLOCK, D_CONV))
        for j in range(CONV_W):
            acc = acc + w[j:j + 1, :] * tap(j, r0)
        mu = jnp.mean(acc, axis=-1, keepdims=True)
        xc = acc - mu
        var = jnp.mean(xc * xc, axis=-1, keepdims=True)
        y = xc * lax.rsqrt(var + LN_EPS) * lg_ref[...] + lb_ref[...]
        c_ref[r0:r0 + CONV_ROW_BLOCK, :] = (y * jax.nn.sigmoid(y)).astype(BF16)

    newhist_ref[...] = full_scr[pad + n:base + n, :]
    if n_steps > 1:
        full_scr[0:base, :] = full_scr[n:n + base, :]


def _conv_branch(u, hist, p, *, n, stride):
    nb, t, _ = u.shape
    w, b, lg, lb = p["conv_w"], p["conv_b"], p["conv_lg"], p["conv_lb"]
    has_hist = hist is not None
    hist_rows = (CONV_W - 1) * stride
    base = hist_rows + (-hist_rows) % SUBLANES_V7X
    tile = pl.BlockSpec((None, n, D_CONV), lambda i, j: (i, j, 0))
    hspec = pl.BlockSpec((None, hist_rows, D_CONV), lambda i, j: (i, 0, 0))
    ins = [u] + ([hist] if has_hist else []) + [w, b, lg, lb]
    in_specs = [tile] + ([hspec] if has_hist else []) + [_layer_spec(a, p["layer"]) for a in (w, b, lg, lb)]
    scratch = [pltpu.VMEM((base + n, D_CONV), F32)]
    if stride % SUBLANES_V7X:
        scratch.append(pltpu.VMEM((SUBLANES_V7X - 1, base + n, D_CONV), F32))
    return pl.pallas_call(
        functools.partial(_conv_kernel, n=n, n_steps=t // n, stride=stride, has_hist=has_hist),
        grid=(nb, t // n),
        in_specs=in_specs,
        out_specs=[tile, hspec],
        out_shape=[jax.ShapeDtypeStruct((nb, t, D_CONV), BF16), jax.ShapeDtypeStruct((nb, hist_rows, D_CONV), F32)],
        scratch_shapes=scratch,
        compiler_params=_cparams(2),
        name="conv_branch",
    )(*ins)


def _wkv_kernel(*refs, chunk, n_chunks, group, has_state):
    if has_state:
        (r_ref, k_ref, v_ref, lw_ref, alr_ref, go_ref, s0_ref, _, kk_ref, ka_ref, rk_ref, lng_ref, lnb_ref,
         y_ref, sout_ref, s_scr) = refs
    else:
        (r_ref, k_ref, v_ref, lw_ref, alr_ref, go_ref, kk_ref, ka_ref, rk_ref, lng_ref, lnb_ref,
         y_ref, sout_ref, s_scr) = refs
    c = chunk
    transposed_state = not has_state

    @pl.when(pl.program_id(1) == 0)
    def _():
        if has_state:
            s_scr[...] = s0_ref[...]
        else:
            s_scr[...] = jnp.zeros_like(s_scr)

    row = lax.broadcasted_iota(jnp.int32, (c, c), 0)
    col = lax.broadcasted_iota(jnp.int32, (c, c), 1)
    incl = row >= col
    strict = row > col
    tri = jnp.where(incl, 1.0, 0.0).astype(BF16)
    sh = lambda x, s: lax.shift_right_logical(x, jnp.int32(s))
    eye = row == col
    first = strict & (sh(row, 1) == sh(col, 1))
    offs = []
    lg = 1
    while (1 << lg) < c:
        offs.append((1 << lg, (sh(row, lg + 1) == sh(col, lg + 1))
                     & ((sh(row, lg) & 1) == 1) & ((sh(col, lg) & 1) == 0)))
        lg += 1

    srow = lax.broadcasted_iota(jnp.int32, (MXU_DIM_V7X, MXU_DIM_V7X), 0)
    scol = lax.broadcasted_iota(jnp.int32, (MXU_DIM_V7X, MXU_DIM_V7X), 1)
    head_lg = HEAD.bit_length() - 1
    blockdiag = jnp.where(sh(srow, head_lg) == sh(scol, head_lg), 1.0, 0.0).astype(BF16)

    def head_sums(xs):
        xb = [x.astype(BF16) for x in xs]
        return [jnp.concatenate([_dot(b[:, q:q + MXU_DIM_V7X], blockdiag) for q in range(0, D_MODEL, MXU_DIM_V7X)],
                                axis=1) for b in xb]

    gs = range(group)
    r = [r_ref[g] for g in gs]
    k = [k_ref[g] for g in gs]
    v = [v_ref[g] for g in gs]
    lw = [lw_ref[g] for g in gs]
    alr = [alr_ref[g] for g in gs]
    kkraw = [k[g] * kk_ref[...] for g in gs]
    kmod = [k[g] * (1.0 + (alr[g] - 1.0) * ka_ref[...]) for g in gs]
    hi = [lw[g].astype(BF16) for g in gs]
    rem = [lw[g] - hi[g].astype(F32) for g in gs]
    mid = [rem[g].astype(BF16) for g in gs]
    lo = [(rem[g] - mid[g].astype(F32)).astype(BF16) for g in gs]
    cum = [_dot(tri, hi[g]) + _dot(tri, mid[g]) + _dot(tri, lo[g]) for g in gs]
    nrm2 = head_sums([kkraw[g] * kkraw[g] for g in gs])
    w_incl = [jnp.exp(cum[g]) for g in gs]
    w_excl = [jnp.exp(cum[g] - lw[g]) for g in gs]
    w_inv = [jnp.exp(-cum[g]) for g in gs]
    w_last = [w_incl[g][c - 1:c, :] for g in gs]
    kk = [kkraw[g] / jnp.maximum(jnp.sqrt(nrm2[g]), 1e-12) for g in gs]
    a_t = [-kk[g] * w_excl[g] for g in gs]
    b_t = [kk[g] * alr[g] * w_inv[g] for g in gs]
    r_t = [r[g] * w_incl[g] for g in gs]
    k_t = [kmod[g] * w_inv[g] for g in gs]
    lhs_f = [jnp.concatenate([a_t[g], r_t[g]], axis=0).astype(BF16) for g in gs]
    rhs_f = [jnp.concatenate([b_t[g], k_t[g]], axis=0) for g in gs]
    v_b = [v[g].astype(BF16) for g in gs]
    lane_groups = range(0, D_MODEL, LANES_V7X)
    if transposed_state:
        rhs_t = [[rhs_f[g][:, q:q + LANES_V7X].T for q in lane_groups] for g in gs]
        decay = [[jnp.broadcast_to(w_last[g][:, q:q + LANES_V7X], (LANES_V7X, LANES_V7X)).T for q in lane_groups]
                 for g in gs]
        rhs_tb = [[x.astype(BF16) for x in rhs_t[g]] for g in gs]
        rhs_wt = [[(x * d[:, :2 * c]).astype(BF16) for x, d in zip(rhs_t[g], decay[g])] for g in gs]
    else:
        rhs_b = [rhs_f[g].astype(BF16) for g in gs]
        rhs_wb = [(rhs_f[g] * w_last[g]).astype(BF16) for g in gs]

    chains = [(g, h) for g in gs for h in range(N_HEADS)]
    sl = lambda h: slice(h * HEAD, (h + 1) * HEAD)
    nt = (((1,), (1,)), ((), ()))
    tn = (((0,), (0,)), ((), ()))
    s_old = [s_scr[g, h] for g, h in chains]
    idx = range(len(chains))
    head_rows = lambda h: slice((h % 2) * HEAD, (h % 2 + 1) * HEAD)
    if transposed_state:
        both = [_dot(lhs_f[g][:, sl(h)],
                     jnp.concatenate([rhs_tb[g][h // 2][head_rows(h), :], s_old[i].astype(BF16)], axis=1))
                for i, (g, h) in enumerate(chains)]
        pm = [both[i][:, :2 * c] for i in idx]
        gm = [both[i][:, 2 * c:] for i in idx]
    else:
        pm = [lax.dot_general(lhs_f[g][:, sl(h)], rhs_b[g][:, sl(h)], nt, preferred_element_type=F32)
              for g, h in chains]
        gm = [lax.dot_general(lhs_f[g][:, sl(h)], s_old[i].astype(BF16), nt, preferred_element_type=F32)
              for i, (g, h) in enumerate(chains)]
    low = [jnp.where(strict, pm[i][:c, :c], 0.0) for i in idx]
    pakv = [_dot(jnp.where(strict, pm[i][:c, c:], 0.0).astype(BF16), v_b[g][:, sl(h)])
            for i, (g, h) in enumerate(chains)]

    def odd_rows(x, s):
        return jnp.concatenate([x[b * s:(b + 1) * s] for b in range(1, c // s, 2)], axis=0)

    def with_odd_rows(x, odd, s):
        return jnp.concatenate([odd[(b // 2) * s:(b // 2 + 1) * s] if b % 2 else x[b * s:(b + 1) * s]
                                for b in range(c // s)], axis=0)

    t = [jnp.where(eye, 1.0, 0.0) + jnp.where(first, low[i], 0.0) for i in idx]
    for s, m in offs:
        tb = [t[i].astype(BF16) for i in idx]
        if s % SUBLANES_V7X:
            a = [_dot(jnp.where(m, low[i], 0.0).astype(BF16), tb[i]) for i in idx]
            t = [t[i] + _dot(tb[i], a[i].astype(BF16)) for i in idx]
        else:
            zero = jnp.zeros((c, c), F32)
            a = [_dot(odd_rows(jnp.where(m, low[i], 0.0), s).astype(BF16), tb[i]) for i in idx]
            t_odd = [odd_rows(t[i], s) for i in idx]
            upd = [_dot(t_odd[i].astype(BF16), with_odd_rows(zero, a[i], s).astype(BF16)) for i in idx]
            t = [with_odd_rows(t[i], t_odd[i] + upd[i], s) for i in idx]
    u = [_dot(t[i].astype(BF16), (gm[i][:c] + pakv[i]).astype(BF16)) for i in idx]
    uv = [jnp.concatenate([u[i].astype(BF16), v_b[g][:, sl(h)]], axis=0) for i, (g, h) in enumerate(chains)]
    m2 = [jnp.concatenate([jnp.where(incl, pm[i][c:, :c], 0.0), jnp.where(incl, pm[i][c:, c:], 0.0)],
                          axis=1).astype(BF16) for i in idx]
    y = [gm[i][c:] + _dot(m2[i], uv[i]) for i in idx]
    for i, (g, h) in enumerate(chains):
        if transposed_state:
            s_scr[g, h] = (s_old[i] * decay[g][h // 2][head_rows(h), :HEAD]
                           + _dot(rhs_wt[g][h // 2][head_rows(h), :], uv[i]))
        else:
            s_scr[g, h] = (s_old[i] * w_last[g][:, sl(h)]
                           + lax.dot_general(uv[i], rhs_wb[g][:, sl(h)], tn, preferred_element_type=F32))

    y_f = [jnp.concatenate(y[g * N_HEADS:(g + 1) * N_HEADS], axis=1) for g in gs]
    mu = head_sums(y_f)
    yc = [y_f[g] - mu[g] * (1.0 / HEAD) for g in gs]
    var_bonus = head_sums([yc[g] * yc[g] for g in gs] + [r[g] * kmod[g] * rk_ref[...] for g in gs])
    for g in gs:
        yn = yc[g] * lax.rsqrt(var_bonus[g] * (1.0 / HEAD) + GN_EPS) * lng_ref[...] + lnb_ref[...]
        y_ref[g] = (yn + var_bonus[group + g] * v[g]) * go_ref[g]

    @pl.when(pl.program_id(1) == n_chunks - 1)
    def _():
        if transposed_state:
            hrow = lax.broadcasted_iota(jnp.int32, (HEAD, HEAD), 0)
            hcol = lax.broadcasted_iota(jnp.int32, (HEAD, HEAD), 1)
            ident = jnp.where(hrow == hcol, 1.0, 0.0).astype(BF16)
            xt = lambda p: lax.dot_general(p, ident, tn, preferred_element_type=F32)
            for g, h in chains:
                st = s_scr[g, h]
                p0 = st.astype(BF16)
                rem0 = st - p0.astype(F32)
                p1 = rem0.astype(BF16)
                p2 = (rem0 - p1.astype(F32)).astype(BF16)
                sout_ref[g, h] = xt(p0) + xt(p1) + xt(p2)
        else:
            sout_ref[...] = s_scr[...]


def _wkv(r, k, v, lw, alr, go, state, p, *, chunk, group):
    nb, t, _ = r.shape
    has_state = state is not None
    assert has_state or 2 * chunk == LANES_V7X
    n_chunks = t // chunk
    tile = pl.BlockSpec((group, chunk, D_MODEL), lambda i, j: (i, j, 0))
    params = (p["kk"], p["ka"], p["rk"], p["lng"], p["lnb"])
    pspecs = [_layer_spec(a, p["layer"]) for a in params]
    kern = functools.partial(_wkv_kernel, chunk=chunk, n_chunks=n_chunks, group=group, has_state=has_state)
    y_shape = jax.ShapeDtypeStruct((nb, t, D_MODEL), F32)
    scratch = [pltpu.VMEM((group, N_HEADS, HEAD, HEAD), F32)]
    if not has_state:
        sspec = pl.BlockSpec((group, N_HEADS, HEAD, HEAD), lambda i, j: (i, 0, 0, 0))
        return pl.pallas_call(
            kern, grid=(nb // group, n_chunks),
            in_specs=[tile] * 6 + pspecs,
            out_specs=[tile, sspec],
            out_shape=[y_shape, jax.ShapeDtypeStruct((nb, N_HEADS, HEAD, HEAD), F32)],
            scratch_shapes=scratch, compiler_params=_cparams(2), name="wkv",
        )(r, k, v, lw, alr, go, *params)
    all_states, out_states, layer = state
    lspec = pl.BlockSpec((None, group, N_HEADS, HEAD, HEAD), lambda i, j: (layer, i, 0, 0, 0))
    in_spec = pl.BlockSpec((group, N_HEADS, HEAD, HEAD), lambda i, j: (i, 0, 0, 0))
    all_states = all_states[layer]
    return pl.pallas_call(
        kern, grid=(nb // group, n_chunks),
        in_specs=[tile] * 6 + [in_spec, pl.BlockSpec(memory_space=pl.ANY)] + pspecs,
        out_specs=[tile, lspec],
        out_shape=[y_shape, jax.ShapeDtypeStruct(out_states.shape, F32)],
        input_output_aliases={7: 1},
        scratch_shapes=scratch, compiler_params=_cparams(2), name="wkv",
    )(r, k, v, lw, alr, go, all_states, out_states, *params)


GELU_C0 = 0.7978845608028654
GELU_C1 = 0.044715


def _back_kernel(*refs, n, stride, has_hist, final_norm):
    refs = list(refs)
    x_ref, c_ref, y_ref, gate_ref = refs[:4]
    del refs[:4]
    hist_ref = refs.pop(0) if has_hist else None
    wco_ref, wro_ref, wmo_ref, gffn_ref, wup_ref, fw_ref, fb_ref, wd_ref = refs[:8]
    del refs[:8]
    gfin_ref = refs.pop(0) if final_norm else None
    o_ref, newhist_ref, full_scr = refs
    hist_rows = (FFN_CONV_W - 1) * stride
    p = max(SUBLANES_V7X, hist_rows)

    @pl.when(pl.program_id(1) == 0)
    def _():
        if has_hist:
            full_scr[0:p, :] = hist_ref[...]
        else:
            full_scr[0:p, :] = jnp.zeros((p, 2 * D_FF), F32)

    ya = _dot(c_ref[...], wco_ref[...])
    yb = _dot(y_ref[...].astype(BF16), wro_ref[...])
    g = gate_ref[...]
    m = g[:, :D_MODEL] * ya + g[:, D_MODEL:] * yb
    x1 = x_ref[...] + _dot(m.astype(BF16), wmo_ref[...])

    h2 = _rmsnorm(x1, gffn_ref[...]).astype(BF16)

    def conv_cols(cols):
        up = _dot(h2, wup_ref[:, cols])
        full_scr[p:p + n, cols] = up
        return (fb_ref[:, cols] + fw_ref[0:1, cols] * full_scr[p - 2 * stride:p - 2 * stride + n, cols]
                + fw_ref[1:2, cols] * full_scr[p - stride:p - stride + n, cols] + fw_ref[2:3, cols] * up)

    pair = lambda q: (conv_cols(slice(q, q + FFN_COL_BLOCK)), conv_cols(slice(D_FF + q, D_FF + q + FFN_COL_BLOCK)))
    x2 = x1
    nxt = pair(0)
    for q in range(0, D_FF, FFN_COL_BLOCK):
        a, gate_lin = nxt
        if q + FFN_COL_BLOCK < D_FF:
            nxt = pair(q + FFN_COL_BLOCK)
        gelu = 0.5 * a * (1.0 + jnp.tanh(GELU_C0 * (a + GELU_C1 * (a * a * a))))
        x2 = x2 + _dot((gelu * gate_lin).astype(BF16), wd_ref[q:q + FFN_COL_BLOCK, :])
    last = full_scr[n:n + p, :]
    newhist_ref[...] = last
    full_scr[0:p, :] = last
    o_ref[...] = _rmsnorm(x2, gfin_ref[...]) if final_norm else x2


def _back(x, c, y, gate, hist, p, final_g, *, n, stride):
    nb, t, _ = x.shape
    has_hist = hist is not None
    final_norm = final_g is not None
    pr = max(SUBLANES_V7X, (FFN_CONV_W - 1) * stride)
    tile = lambda w: pl.BlockSpec((None, n, w), lambda i, j: (i, j, 0))
    hspec = pl.BlockSpec((None, pr, 2 * D_FF), lambda i, j: (i, 0, 0))
    params = [p["wco"], p["wro"], p["wmo"], p["norm_ffn_g"], p["wup"], p["ffn_w"], p["ffn_b"], p["wdown"]]
    ins = [x, c, y, gate] + ([hist] if has_hist else []) + params + ([final_g] if final_norm else [])
    in_specs = ([tile(D_MODEL), tile(D_CONV), tile(D_MODEL), tile(2 * D_MODEL)] + ([hspec] if has_hist else [])
                + [_layer_spec(a, p["layer"]) for a in params]
                + ([_const_spec(final_g.shape)] if final_norm else []))
    return pl.pallas_call(
        functools.partial(_back_kernel, n=n, stride=stride, has_hist=has_hist, final_norm=final_norm),
        grid=(nb, t // n),
        in_specs=in_specs,
        out_specs=[tile(D_MODEL), hspec],
        out_shape=[jax.ShapeDtypeStruct((nb, t, D_MODEL), F32), jax.ShapeDtypeStruct((nb, pr, 2 * D_FF), F32)],
        scratch_shapes=[pltpu.VMEM((pr + n, 2 * D_FF), F32)],
        compiler_params=_cparams(2),
        name="back",
    )(*ins)


def _pad_axis(a, axis, size):
    widths = [(0, 0)] * a.ndim
    widths[axis] = (0, size - a.shape[axis])
    return jnp.pad(a, widths)


def _regroup_lora_cols(a):
    wl = a[..., 0:LORA_DECAY]
    al = a[..., LORA_DECAY:LORA_DECAY + LORA_AAA]
    gl = a[..., LORA_DECAY + LORA_AAA:]
    return jnp.concatenate([_pad_axis(wl, -1, LANES_V7X), _pad_axis(al, -1, LANES_V7X),
                            _pad_axis(gl, -1, LORA_PAD - LORA_G_OFF)], axis=-1)


def _ungroup_lora_cols(a):
    return jnp.concatenate([a[..., LORA_W_OFF:LORA_W_OFF + LORA_DECAY], a[..., LORA_A_OFF:LORA_A_OFF + LORA_AAA],
                            a[..., LORA_G_OFF:LORA_G_OFF + LORA_GATE]], axis=-1)


def _stack_weights(norm_mix_g, w_in, conv_dw_w, conv_dw_b, conv_ln_g, conv_ln_b, w_conv_out,
                   rw_mu, rw_w0, rw_w2, rw_a0, rw_a2, rw_g2, rw_k_k, rw_k_a, rw_r_k, rw_ln_g, rw_ln_b, w_rw_out,
                   w_mix_out, norm_ffn_g, w_up, ffn_dw_w, ffn_dw_b, w_down):
    row = lambda a: a.reshape(a.shape[0], 1, -1)
    o = 2 * D_CONV
    mu = row(rw_mu)
    return dict(
        norm_mix_g=row(norm_mix_g),
        wc=w_in[:, :, :o].astype(BF16),
        wrkv=w_in[:, :, o:o + D_RKV].astype(BF16),
        wlora=_regroup_lora_cols(w_in[:, :, o + D_RKV:o + D_RW_IN]).astype(BF16),
        wg=w_in[:, :, o + D_RW_IN:].astype(BF16),
        conv_w=conv_dw_w, conv_b=row(conv_dw_b), conv_lg=row(conv_ln_g), conv_lb=row(conv_ln_b),
        wco=w_conv_out.astype(BF16),
        murkv=mu[:, :, :D_RKV], mulora=_regroup_lora_cols(mu[:, :, D_RKV:]),
        w0=row(rw_w0), w2=_pad_axis(rw_w2, 1, LANES_V7X).astype(BF16),
        a0=row(rw_a0), a2=_pad_axis(rw_a2, 1, LANES_V7X).astype(BF16),
        g2=_pad_axis(rw_g2, 1, LORA_PAD - LORA_G_OFF).astype(BF16),
        kk=row(rw_k_k), ka=row(rw_k_a), rk=row(rw_r_k), lng=row(rw_ln_g), lnb=row(rw_ln_b),
        wro=w_rw_out.astype(BF16), wmo=w_mix_out.astype(BF16),
        norm_ffn_g=row(norm_ffn_g), wup=w_up.astype(BF16),
        ffn_w=ffn_dw_w, ffn_b=row(ffn_dw_b), wdown=w_down.astype(BF16),
    )


def _layer(x, p, final_g, *, stride, chunk, group, conv_n, conv_hist, shift_state, wkv_state, ffn_hist, wkv_layout):
    srkv, slora = shift_state if shift_state is not None else (None, None)
    u, gate, r, k, v, lw, alr, go, nrkv, nlora = _front(x, srkv, slora, p, n=min(FRONT_TILE, x.shape[1]),
                                                       stride=stride)
    c, new_conv = _conv_branch(u, conv_hist, p, n=conv_n, stride=stride)
    to_seq, from_seq = wkv_layout
    y, new_wkv = _wkv(*(to_seq(a) for a in (r, k, v, lw, alr, go)), wkv_state, p, chunk=chunk, group=group)
    x2, new_ffn = _back(x, c, from_seq(y), gate, ffn_hist, p, final_g, n=min(BACK_TILE, x.shape[1]),
                        stride=stride)
    return x2, new_conv, nrkv, nlora, new_wkv, new_ffn


def kernel(x_prompt, x_sample, state_conv, state_shift, state_wkv, state_ffn,
           norm_mix_g, w_in, conv_dw_w, conv_dw_b, conv_ln_g, conv_ln_b, w_conv_out,
           rw_mu, rw_w0, rw_w2, rw_a0, rw_a2, rw_g2, rw_k_k, rw_k_a, rw_r_k, rw_ln_g, rw_ln_b, w_rw_out,
           w_mix_out, norm_ffn_g, w_up, ffn_dw_w, ffn_dw_b, w_down, norm_final_g):
    params = (norm_mix_g, w_in, conv_dw_w, conv_dw_b, conv_ln_g, conv_ln_b, w_conv_out,
              rw_mu, rw_w0, rw_w2, rw_a0, rw_a2, rw_g2, rw_k_k, rw_k_a, rw_r_k, rw_ln_g, rw_ln_b, w_rw_out,
              w_mix_out, norm_ffn_g, w_up, ffn_dw_w, ffn_dw_b, w_down)
    depth = w_in.shape[0]
    b_p, t_p, _ = x_prompt.shape
    b_s, t_s, _ = x_sample.shape
    final_g = norm_final_g.reshape(1, -1)

    ns = SAMPLE_SPLIT
    stride = b_s // ns

    def tm(a):
        a = jnp.transpose(a.reshape(ns, stride, a.shape[1], a.shape[2]), (0, 2, 1, 3))
        return a.reshape(ns, a.shape[1] * stride, a.shape[3])

    def untm(a, rows):
        a = jnp.transpose(a.reshape(ns, rows, stride, a.shape[-1]), (0, 2, 1, 3))
        return a.reshape(b_s, rows, a.shape[-1])

    xp = x_prompt
    xs = tm(x_sample)

    ident = lambda a: a
    prompt_layout = (ident, ident)
    sample_to_seq = lambda a: jnp.pad(untm(a, t_s), ((0, 0), (0, SAMPLE_CHUNK - t_s), (0, 0)))
    sample_from_seq = lambda a: tm(a[:, :t_s])

    p_conv, p_shift, p_wkv, p_ffn = [], [], [], []
    s_conv, s_shift, s_ffn = [], [], []
    s_wkv = jnp.zeros(state_wkv.shape, F32)
    stacked = _stack_weights(*params)
    for l in range(depth):
        p = dict(stacked, layer=l)
        fin = final_g if l == depth - 1 else None

        xp, nc, nrkv, nlora, nw, nf = _layer(
            xp, p, fin, stride=1, chunk=PROMPT_CHUNK, group=PROMPT_GROUP, conv_n=CONV_TILE, conv_hist=None,
            shift_state=None, wkv_state=None, ffn_hist=None, wkv_layout=prompt_layout)
        p_conv.append(nc)
        p_shift.append(jnp.concatenate([nrkv[:, -1], _ungroup_lora_cols(nlora[:, -1])], axis=-1))
        p_wkv.append(nw)
        p_ffn.append(nf[:, nf.shape[1] - (FFN_CONV_W - 1):])

        sh = state_shift[l]
        xs, nc, nrkv, nlora, s_wkv, nf = _layer(
            xs, p, fin, stride=stride, chunk=SAMPLE_CHUNK, group=SAMPLE_GROUP, conv_n=t_s * stride,
            conv_hist=tm(state_conv[l]),
            shift_state=(sh[:, :D_RKV].reshape(ns, stride, D_RKV),
                         _regroup_lora_cols(sh[:, D_RKV:]).reshape(ns, stride, LORA_PAD)),
            wkv_state=(state_wkv, s_wkv, l), ffn_hist=tm(state_ffn[l]),
            wkv_layout=(sample_to_seq, sample_from_seq))
        s_conv.append(untm(nc, CONV_W - 1))
        s_shift.append(jnp.concatenate([nrkv.reshape(b_s, D_RKV), _ungroup_lora_cols(nlora.reshape(b_s, LORA_PAD))],
                                       axis=-1))
        s_ffn.append(untm(nf, FFN_CONV_W - 1))

    y_prompt = xp
    y_sample = untm(xs, t_s)
    return (y_prompt, y_sample, jnp.stack(p_conv), jnp.stack(p_shift), jnp.stack(p_wkv), jnp.stack(p_ffn),
            jnp.stack(s_conv), jnp.stack(s_shift), s_wkv, jnp.stack(s_ffn))
```

```python
import functools

import jax
import jax.numpy as jnp
from jax import lax
from jax.experimental import pallas as pl
from jax.experimental.pallas import tpu as pltpu

F32 = jnp.float32
BF16 = jnp.bfloat16

D_MODEL = 1024
D_CONV = D_MODEL // 2
CONV_W = 31
HEAD = 64
N_HEADS = D_MODEL // HEAD
LORA_DECAY = 64
LORA_AAA = 64
LORA_GATE = 160
D_FF = 3 * D_MODEL
FFN_CONV_W = 3
D_RKV = 3 * D_MODEL
RW_PARTS = 6
D_RW_IN = D_RKV + LORA_DECAY + LORA_AAA + LORA_GATE
RMS_EPS = 1e-6
LN_EPS = 1e-5
GN_EPS = 64e-5

SUBLANES_V7X = 8
LANES_V7X = 128
MXU_DIM_V7X = 256
VMEM_LIMIT_BYTES_V7X = 56 * 1024 * 1024

LORA_PAD = 4 * LANES_V7X
LORA_W_OFF, LORA_A_OFF, LORA_G_OFF = 0, LANES_V7X, 2 * LANES_V7X

BACK_TILE = 256
FFN_COL_BLOCK = 256
FRONT_TILE = 256
CONV_TILE = 256
CONV_ROW_BLOCK = 64
PROMPT_CHUNK = 64
SAMPLE_CHUNK = 8
PROMPT_GROUP = 4
SAMPLE_GROUP = 8
SAMPLE_SPLIT = 4


def _cparams(n_grid):
    return pltpu.CompilerParams(dimension_semantics=("arbitrary",) * n_grid,
                                vmem_limit_bytes=VMEM_LIMIT_BYTES_V7X)


def _const_spec(shape):
    return pl.BlockSpec(shape, lambda *_: (0,) * len(shape), pipeline_mode=pl.Buffered(1))


def _layer_spec(a, layer):
    return pl.BlockSpec((None,) + a.shape[1:], lambda *_: (layer,) + (0,) * (a.ndim - 1),
                        pipeline_mode=pl.Buffered(1))


def _dot(a, b):
    return jnp.dot(a, b, preferred_element_type=F32)


def _rmsnorm(x, g):
    return x * lax.rsqrt(jnp.mean(x * x, axis=-1, keepdims=True) + RMS_EPS) * g


def _front_kernel(*refs, n, stride, has_state):
    if has_state:
        (x_ref, srkv_ref, slora_ref, g_ref, wc_ref, wrkv_ref, wlora_ref, wg_ref, murkv_ref, mulora_ref,
         w0_ref, w2_ref, a0_ref, a2_ref, g2_ref,
         u_ref, gate_ref, rw_ref, nrkv_ref, nlora_ref, frkv_scr, flora_scr) = refs
    else:
        (x_ref, g_ref, wc_ref, wrkv_ref, wlora_ref, wg_ref, murkv_ref, mulora_ref,
         w0_ref, w2_ref, a0_ref, a2_ref, g2_ref,
         u_ref, gate_ref, rw_ref, nrkv_ref, nlora_ref, frkv_scr, flora_scr) = refs
    p = max(SUBLANES_V7X, stride)
    rw = lambda i: rw_ref.at[:, i * D_MODEL:(i + 1) * D_MODEL]
    r_ref, k_ref, v_ref, lw_ref, alr_ref, go_ref = (rw(i) for i in range(RW_PARTS))

    @pl.when(pl.program_id(1) == 0)
    def _():
        if has_state:
            frkv_scr[0:p, :] = srkv_ref[...]
            flora_scr[0:p, :] = slora_ref[...]
        else:
            frkv_scr[0:p, :] = jnp.zeros((p, D_RKV), F32)
            flora_scr[0:p, :] = jnp.zeros((p, LORA_PAD), F32)

    hb = _rmsnorm(x_ref[...], g_ref[...]).astype(BF16)
    zlora = _dot(hb, wlora_ref[...])
    zrkv = _dot(hb, wrkv_ref[...])

    flora_scr[p:p + n, :] = zlora
    xl = zlora + (flora_scr[p - stride:p - stride + n, :] - zlora) * mulora_ref[...]
    last_lora = flora_scr[n:n + p, :]
    nlora_ref[...] = last_lora
    flora_scr[0:p, :] = last_lora
    wl = jnp.tanh(xl[:, LORA_W_OFF:LORA_W_OFF + LANES_V7X]).astype(BF16)
    al = xl[:, LORA_A_OFF:LORA_A_OFF + LANES_V7X].astype(BF16)
    gl = jax.nn.sigmoid(xl[:, LORA_G_OFF:LORA_PAD]).astype(BF16)
    t = -(w0_ref[...] + _dot(wl, w2_ref[...]))
    a_lin = a0_ref[...] + _dot(al, a2_ref[...])
    go_ref[...] = _dot(gl, g2_ref[...])
    zc = _dot(hb, wc_ref[...])
    zg = _dot(hb, wg_ref[...])

    softplus = jnp.maximum(t, 0.0) + jnp.log(1.0 + jnp.exp(-jnp.abs(t)))
    lw_ref[...] = -jnp.exp(-softplus - 0.5)
    alr_ref[...] = jax.nn.sigmoid(a_lin)

    frkv_scr[p:p + n, :] = zrkv
    xs = zrkv + (frkv_scr[p - stride:p - stride + n, :] - zrkv) * murkv_ref[...]
    last_rkv = frkv_scr[n:n + p, :]
    nrkv_ref[...] = last_rkv
    frkv_scr[0:p, :] = last_rkv
    r_ref[...] = xs[:, 0:D_MODEL]
    k_ref[...] = xs[:, D_MODEL:2 * D_MODEL]
    v_ref[...] = xs[:, 2 * D_MODEL:3 * D_MODEL]

    u_ref[...] = zc[:, :D_CONV] * jax.nn.sigmoid(zc[:, D_CONV:])
    gate_ref[...] = jax.nn.sigmoid(zg)


def _front(x, shift_state, p, *, n, stride):
    nb, t, _ = x.shape
    has_state = shift_state is not None
    layer = p["layer"]
    pr = max(SUBLANES_V7X, stride)
    tile = lambda c: pl.BlockSpec((None, n, c), lambda i, j: (i, j, 0))
    state = lambda c: pl.BlockSpec((None, pr, c), lambda i, j: (i, 0, 0))
    state_in = lambda c: pl.BlockSpec((None, None, pr, c), lambda i, j: (layer, i, 0, 0))
    params = (p["norm_mix_g"], p["wc"], p["wrkv"], p["wlora"], p["wg"], p["murkv"], p["mulora"],
              p["w0"], p["w2"], p["a0"], p["a2"], p["g2"])
    ins = [x] + (list(shift_state) if has_state else []) + list(params)
    in_specs = ([tile(D_MODEL)] + ([state_in(D_RKV), state_in(LORA_PAD)] if has_state else [])
                + [_layer_spec(a, layer) for a in params])
    full = lambda c: jax.ShapeDtypeStruct((nb, t, c), F32)
    return pl.pallas_call(
        functools.partial(_front_kernel, n=n, stride=stride, has_state=has_state),
        grid=(nb, t // n),
        in_specs=in_specs,
        out_specs=[tile(D_CONV), tile(2 * D_MODEL), tile(RW_PARTS * D_MODEL), state(D_RKV), state(LORA_PAD)],
        out_shape=[full(D_CONV), full(2 * D_MODEL), full(RW_PARTS * D_MODEL),
                   jax.ShapeDtypeStruct((nb, pr, D_RKV), F32), jax.ShapeDtypeStruct((nb, pr, LORA_PAD), F32)],
        scratch_shapes=[pltpu.VMEM((pr + n, D_RKV), F32), pltpu.VMEM((pr + n, LORA_PAD), F32)],
        compiler_params=_cparams(2),
        name="front",
    )(*ins)


def _conv_kernel(*refs, n, n_steps, stride, has_hist):
    refs = list(refs)
    u_ref = refs.pop(0)
    hist_ref = refs.pop(0) if has_hist else None
    w_ref, b_ref, lg_ref, lb_ref, c_ref, newhist_ref, full_scr = refs[:7]
    aligned = stride % SUBLANES_V7X == 0
    shift_scr = None if aligned else refs[7]
    hist_rows = (CONV_W - 1) * stride
    pad = (-hist_rows) % SUBLANES_V7X
    base = pad + hist_rows

    @pl.when(pl.program_id(1) == 0)
    def _():
        if has_hist:
            if pad:
                full_scr[0:pad, :] = jnp.zeros((pad, D_CONV), F32)
            full_scr[pad:base, :] = hist_ref[...]
        else:
            full_scr[0:base, :] = jnp.zeros((base, D_CONV), F32)

    full_scr[base:base + n, :] = u_ref[...]
    if not aligned:
        span = base + n - SUBLANES_V7X
        for s in range(1, SUBLANES_V7X):
            shift_scr[s - 1, 0:span, :] = full_scr[s:s + span, :]

    def tap(j, r0):
        o = pad + j * stride
        s = o % SUBLANES_V7X
        a = r0 + o - s
        if s == 0:
            return full_scr[a:a + CONV_ROW_BLOCK, :]
        return shift_scr[s - 1, a:a + CONV_ROW_BLOCK, :]

    w = w_ref[...]
    for rb in range(n // CONV_ROW_BLOCK):
        r0 = rb * CONV_ROW_BLOCK
        acc = jnp.broadcast_to(b_ref[...], (CONV_ROW_BLOCK, D_CONV))
        for j in range(CONV_W):
            acc = acc + w[j:j + 1, :] * tap(j, r0)
        mu = jnp.mean(acc, axis=-1, keepdims=True)
        xc = acc - mu
        var = jnp.mean(xc * xc, axis=-1, keepdims=True)
        y = xc * lax.rsqrt(var + LN_EPS) * lg_ref[...] + lb_ref[...]
        c_ref[r0:r0 + CONV_ROW_BLOCK, :] = (y * jax.nn.sigmoid(y)).astype(BF16)

    newhist_ref[...] = full_scr[pad + n:base + n, :]
    if n_steps > 1:
        full_scr[0:base, :] = full_scr[n:n + base, :]


def _conv_branch(u, hist, p, *, n, stride):
    nb, t, _ = u.shape
    w, b, lg, lb = p["conv_w"], p["conv_b"], p["conv_lg"], p["conv_lb"]
    has_hist = hist is not None
    hist_rows = (CONV_W - 1) * stride
    base = hist_rows + (-hist_rows) % SUBLANES_V7X
    tile = pl.BlockSpec((None, n, D_CONV), lambda i, j: (i, j, 0))
    layer = p["layer"]
    hspec = pl.BlockSpec((None, hist_rows, D_CONV), lambda i, j: (i, 0, 0))
    hspec_in = pl.BlockSpec((None, None, hist_rows, D_CONV), lambda i, j: (layer, i, 0, 0))
    ins = [u] + ([hist] if has_hist else []) + [w, b, lg, lb]
    in_specs = [tile] + ([hspec_in] if has_hist else []) + [_layer_spec(a, layer) for a in (w, b, lg, lb)]
    scratch = [pltpu.VMEM((base + n, D_CONV), F32)]
    if stride % SUBLANES_V7X:
        scratch.append(pltpu.VMEM((SUBLANES_V7X - 1, base + n, D_CONV), F32))
    return pl.pallas_call(
        functools.partial(_conv_kernel, n=n, n_steps=t // n, stride=stride, has_hist=has_hist),
        grid=(nb, t // n),
        in_specs=in_specs,
        out_specs=[tile, hspec],
        out_shape=[jax.ShapeDtypeStruct((nb, t, D_CONV), BF16), jax.ShapeDtypeStruct((nb, hist_rows, D_CONV), F32)],
        scratch_shapes=scratch,
        compiler_params=_cparams(2),
        name="conv_branch",
    )(*ins)


def _wkv_kernel(*refs, chunk, n_chunks, group, has_state):
    if has_state:
        rw_ref, s0_ref, _, kk_ref, ka_ref, rk_ref, lng_ref, lnb_ref, y_ref, sout_ref, s_scr = refs
    else:
        rw_ref, kk_ref, ka_ref, rk_ref, lng_ref, lnb_ref, y_ref, sout_ref, s_scr = refs
    rw = lambda i: rw_ref.at[:, :, i * D_MODEL:(i + 1) * D_MODEL]
    r_ref, k_ref, v_ref, lw_ref, alr_ref, go_ref = (rw(i) for i in range(RW_PARTS))
    c = chunk
    transposed_state = not has_state

    @pl.when(pl.program_id(1) == 0)
    def _():
        if has_state:
            s_scr[...] = s0_ref[...]
        else:
            s_scr[...] = jnp.zeros_like(s_scr)

    row = lax.broadcasted_iota(jnp.int32, (c, c), 0)
    col = lax.broadcasted_iota(jnp.int32, (c, c), 1)
    incl = row >= col
    strict = row > col
    tri = jnp.where(incl, 1.0, 0.0).astype(BF16)
    sh = lambda x, s: lax.shift_right_logical(x, jnp.int32(s))
    eye = row == col
    first = strict & (sh(row, 1) == sh(col, 1))
    offs = []
    lg = 1
    while (1 << lg) < c:
        offs.append((1 << lg, (sh(row, lg + 1) == sh(col, lg + 1))
                     & ((sh(row, lg) & 1) == 1) & ((sh(col, lg) & 1) == 0)))
        lg += 1

    srow = lax.broadcasted_iota(jnp.int32, (MXU_DIM_V7X, MXU_DIM_V7X), 0)
    scol = lax.broadcasted_iota(jnp.int32, (MXU_DIM_V7X, MXU_DIM_V7X), 1)
    head_lg = HEAD.bit_length() - 1
    blockdiag = jnp.where(sh(srow, head_lg) == sh(scol, head_lg), 1.0, 0.0).astype(BF16)

    def head_sums(xs):
        xb = [x.astype(BF16) for x in xs]
        return [jnp.concatenate([_dot(b[:, q:q + MXU_DIM_V7X], blockdiag) for q in range(0, D_MODEL, MXU_DIM_V7X)],
                                axis=1) for b in xb]

    gs = range(group)
    r = [r_ref[g] for g in gs]
    k = [k_ref[g] for g in gs]
    v = [v_ref[g] for g in gs]
    lw = [lw_ref[g] for g in gs]
    alr = [alr_ref[g] for g in gs]
    kkraw = [k[g] * kk_ref[...] for g in gs]
    kmod = [k[g] * (1.0 + (alr[g] - 1.0) * ka_ref[...]) for g in gs]
    hi = [lw[g].astype(BF16) for g in gs]
    rem = [lw[g] - hi[g].astype(F32) for g in gs]
    mid = [rem[g].astype(BF16) for g in gs]
    lo = [(rem[g] - mid[g].astype(F32)).astype(BF16) for g in gs]
    cum = [_dot(tri, hi[g]) + _dot(tri, mid[g]) + _dot(tri, lo[g]) for g in gs]
    nrm2 = head_sums([kkraw[g] * kkraw[g] for g in gs])
    w_incl = [jnp.exp(cum[g]) for g in gs]
    w_excl = [jnp.exp(cum[g] - lw[g]) for g in gs]
    w_inv = [jnp.exp(-cum[g]) for g in gs]
    w_last = [w_incl[g][c - 1:c, :] for g in gs]
    kk = [kkraw[g] / jnp.maximum(jnp.sqrt(nrm2[g]), 1e-12) for g in gs]
    a_t = [-kk[g] * w_excl[g] for g in gs]
    b_t = [kk[g] * alr[g] * w_inv[g] for g in gs]
    r_t = [r[g] * w_incl[g] for g in gs]
    k_t = [kmod[g] * w_inv[g] for g in gs]
    lhs_f = [jnp.concatenate([a_t[g], r_t[g]], axis=0).astype(BF16) for g in gs]
    rhs_f = [jnp.concatenate([b_t[g], k_t[g]], axis=0) for g in gs]
    v_b = [v[g].astype(BF16) for g in gs]
    lane_groups = range(0, D_MODEL, LANES_V7X)
    if transposed_state:
        rhs_t = [[rhs_f[g][:, q:q + LANES_V7X].T for q in lane_groups] for g in gs]
        decay = [[jnp.broadcast_to(w_last[g][:, q:q + LANES_V7X], (LANES_V7X, LANES_V7X)).T for q in lane_groups]
                 for g in gs]
        rhs_tb = [[x.astype(BF16) for x in rhs_t[g]] for g in gs]
        rhs_wt = [[(x * d[:, :2 * c]).astype(BF16) for x, d in zip(rhs_t[g], decay[g])] for g in gs]
    else:
        rhs_b = [rhs_f[g].astype(BF16) for g in gs]
        rhs_wb = [(rhs_f[g] * w_last[g]).astype(BF16) for g in gs]

    chains = [(g, h) for g in gs for h in range(N_HEADS)]
    sl = lambda h: slice(h * HEAD, (h + 1) * HEAD)
    nt = (((1,), (1,)), ((), ()))
    tn = (((0,), (0,)), ((), ()))
    s_old = [s_scr[g, h] for g, h in chains]
    idx = range(len(chains))
    head_rows = lambda h: slice((h % 2) * HEAD, (h % 2 + 1) * HEAD)
    if transposed_state:
        both = [_dot(lhs_f[g][:, sl(h)],
                     jnp.concatenate([rhs_tb[g][h // 2][head_rows(h), :], s_old[i].astype(BF16)], axis=1))
                for i, (g, h) in enumerate(chains)]
        pm = [both[i][:, :2 * c] for i in idx]
        gm = [both[i][:, 2 * c:] for i in idx]
    else:
        pm = [lax.dot_general(lhs_f[g][:, sl(h)], rhs_b[g][:, sl(h)], nt, preferred_element_type=F32)
              for g, h in chains]
        gm = [lax.dot_general(lhs_f[g][:, sl(h)], s_old[i].astype(BF16), nt, preferred_element_type=F32)
              for i, (g, h) in enumerate(chains)]
    low = [jnp.where(strict, pm[i][:c, :c], 0.0) for i in idx]
    pakv = [_dot(jnp.where(strict, pm[i][:c, c:], 0.0).astype(BF16), v_b[g][:, sl(h)])
            for i, (g, h) in enumerate(chains)]

    def odd_rows(x, s):
        return jnp.concatenate([x[b * s:(b + 1) * s] for b in range(1, c // s, 2)], axis=0)

    def with_odd_rows(x, odd, s):
        return jnp.concatenate([odd[(b // 2) * s:(b // 2 + 1) * s] if b % 2 else x[b * s:(b + 1) * s]
                                for b in range(c // s)], axis=0)

    t = [jnp.where(eye, 1.0, 0.0) + jnp.where(first, low[i], 0.0) for i in idx]
    for s, m in offs:
        tb = [t[i].astype(BF16) for i in idx]
        if s % SUBLANES_V7X:
            a = [_dot(jnp.where(m, low[i], 0.0).astype(BF16), tb[i]) for i in idx]
            t = [t[i] + _dot(tb[i], a[i].astype(BF16)) for i in idx]
        else:
            zero = jnp.zeros((c, c), F32)
            a = [_dot(odd_rows(jnp.where(m, low[i], 0.0), s).astype(BF16), tb[i]) for i in idx]
            t_odd = [odd_rows(t[i], s) for i in idx]
            upd = [_dot(t_odd[i].astype(BF16), with_odd_rows(zero, a[i], s).astype(BF16)) for i in idx]
            t = [with_odd_rows(t[i], t_odd[i] + upd[i], s) for i in idx]
    u = [_dot(t[i].astype(BF16), (gm[i][:c] + pakv[i]).astype(BF16)) for i in idx]
    uv = [jnp.concatenate([u[i].astype(BF16), v_b[g][:, sl(h)]], axis=0) for i, (g, h) in enumerate(chains)]
    m2 = [jnp.concatenate([jnp.where(incl, pm[i][c:, :c], 0.0), jnp.where(incl, pm[i][c:, c:], 0.0)],
                          axis=1).astype(BF16) for i in idx]
    y = [gm[i][c:] + _dot(m2[i], uv[i]) for i in idx]
    for i, (g, h) in enumerate(chains):
        if transposed_state:
            s_scr[g, h] = (s_old[i] * decay[g][h // 2][head_rows(h), :HEAD]
                           + _dot(rhs_wt[g][h // 2][head_rows(h), :], uv[i]))
        else:
            s_scr[g, h] = (s_old[i] * w_last[g][:, sl(h)]
                           + lax.dot_general(uv[i], rhs_wb[g][:, sl(h)], tn, preferred_element_type=F32))

    y_f = [jnp.concatenate(y[g * N_HEADS:(g + 1) * N_HEADS], axis=1) for g in gs]
    mu = head_sums(y_f)
    yc = [y_f[g] - mu[g] * (1.0 / HEAD) for g in gs]
    var_bonus = head_sums([yc[g] * yc[g] for g in gs] + [r[g] * kmod[g] * rk_ref[...] for g in gs])
    for g in gs:
        yn = yc[g] * lax.rsqrt(var_bonus[g] * (1.0 / HEAD) + GN_EPS) * lng_ref[...] + lnb_ref[...]
        y_ref[g] = (yn + var_bonus[group + g] * v[g]) * go_ref[g]

    @pl.when(pl.program_id(1) == n_chunks - 1)
    def _():
        if transposed_state:
            hrow = lax.broadcasted_iota(jnp.int32, (HEAD, HEAD), 0)
            hcol = lax.broadcasted_iota(jnp.int32, (HEAD, HEAD), 1)
            ident = jnp.where(hrow == hcol, 1.0, 0.0).astype(BF16)
            xt = lambda p: lax.dot_general(p, ident, tn, preferred_element_type=F32)
            for g, h in chains:
                st = s_scr[g, h]
                p0 = st.astype(BF16)
                rem0 = st - p0.astype(F32)
                p1 = rem0.astype(BF16)
                p2 = (rem0 - p1.astype(F32)).astype(BF16)
                sout_ref[g, h] = xt(p0) + xt(p1) + xt(p2)
        else:
            sout_ref[...] = s_scr[...]


def _wkv(rw, state, p, *, chunk, group):
    nb, t, _ = rw.shape
    has_state = state is not None
    assert has_state or 2 * chunk == LANES_V7X
    n_chunks = t // chunk
    tile = pl.BlockSpec((group, chunk, D_MODEL), lambda i, j: (i, j, 0))
    tile_in = pl.BlockSpec((group, chunk, RW_PARTS * D_MODEL), lambda i, j: (i, j, 0))
    params = (p["kk"], p["ka"], p["rk"], p["lng"], p["lnb"])
    pspecs = [_layer_spec(a, p["layer"]) for a in params]
    kern = functools.partial(_wkv_kernel, chunk=chunk, n_chunks=n_chunks, group=group, has_state=has_state)
    y_shape = jax.ShapeDtypeStruct((nb, t, D_MODEL), F32)
    scratch = [pltpu.VMEM((group, N_HEADS, HEAD, HEAD), F32)]
    if not has_state:
        sspec = pl.BlockSpec((group, N_HEADS, HEAD, HEAD), lambda i, j: (i, 0, 0, 0))
        return pl.pallas_call(
            kern, grid=(nb // group, n_chunks),
            in_specs=[tile_in] + pspecs,
            out_specs=[tile, sspec],
            out_shape=[y_shape, jax.ShapeDtypeStruct((nb, N_HEADS, HEAD, HEAD), F32)],
            scratch_shapes=scratch, compiler_params=_cparams(2), name="wkv",
        )(rw, *params)
    all_states, out_states, layer = state
    lspec = pl.BlockSpec((None, group, N_HEADS, HEAD, HEAD), lambda i, j: (layer, i, 0, 0, 0))
    in_spec = pl.BlockSpec((group, N_HEADS, HEAD, HEAD), lambda i, j: (i, 0, 0, 0))
    all_states = all_states[layer]
    return pl.pallas_call(
        kern, grid=(nb // group, n_chunks),
        in_specs=[tile_in, in_spec, pl.BlockSpec(memory_space=pl.ANY)] + pspecs,
        out_specs=[tile, lspec],
        out_shape=[y_shape, jax.ShapeDtypeStruct(out_states.shape, F32)],
        input_output_aliases={2: 1},
        scratch_shapes=scratch, compiler_params=_cparams(2), name="wkv",
    )(rw, all_states, out_states, *params)


GELU_C0 = 0.7978845608028654
GELU_C1 = 0.044715


def _back_kernel(*refs, n, stride, has_hist, final_norm):
    refs = list(refs)
    x_ref, c_ref, y_ref, gate_ref = refs[:4]
    del refs[:4]
    hist_ref = refs.pop(0) if has_hist else None
    wco_ref, wro_ref, wmo_ref, gffn_ref, wup_ref, fw_ref, fb_ref, wd_ref = refs[:8]
    del refs[:8]
    gfin_ref = refs.pop(0) if final_norm else None
    o_ref, newhist_ref, full_scr = refs
    hist_rows = (FFN_CONV_W - 1) * stride
    p = max(SUBLANES_V7X, hist_rows)

    @pl.when(pl.program_id(1) == 0)
    def _():
        if has_hist:
            full_scr[0:p, :] = hist_ref[...]
        else:
            full_scr[0:p, :] = jnp.zeros((p, 2 * D_FF), F32)

    ya = _dot(c_ref[...], wco_ref[...])
    yb = _dot(y_ref[...].astype(BF16), wro_ref[...])
    g = gate_ref[...]
    m = g[:, :D_MODEL] * ya + g[:, D_MODEL:] * yb
    x1 = x_ref[...] + _dot(m.astype(BF16), wmo_ref[...])

    h2 = _rmsnorm(x1, gffn_ref[...]).astype(BF16)

    def conv_cols(cols):
        up = _dot(h2, wup_ref[:, cols])
        full_scr[p:p + n, cols] = up
        return (fb_ref[:, cols] + fw_ref[0:1, cols] * full_scr[p - 2 * stride:p - 2 * stride + n, cols]
                + fw_ref[1:2, cols] * full_scr[p - stride:p - stride + n, cols] + fw_ref[2:3, cols] * up)

    pair = lambda q: (conv_cols(slice(q, q + FFN_COL_BLOCK)), conv_cols(slice(D_FF + q, D_FF + q + FFN_COL_BLOCK)))
    x2 = x1
    nxt = pair(0)
    for q in range(0, D_FF, FFN_COL_BLOCK):
        a, gate_lin = nxt
        if q + FFN_COL_BLOCK < D_FF:
            nxt = pair(q + FFN_COL_BLOCK)
        gelu = 0.5 * a * (1.0 + jnp.tanh(GELU_C0 * (a + GELU_C1 * (a * a * a))))
        x2 = x2 + _dot((gelu * gate_lin).astype(BF16), wd_ref[q:q + FFN_COL_BLOCK, :])
    last = full_scr[n:n + p, :]
    newhist_ref[...] = last
    full_scr[0:p, :] = last
    o_ref[...] = _rmsnorm(x2, gfin_ref[...]) if final_norm else x2


def _back(x, c, y, gate, hist, p, final_g, *, n, stride):
    nb, t, _ = x.shape
    has_hist = hist is not None
    final_norm = final_g is not None
    pr = max(SUBLANES_V7X, (FFN_CONV_W - 1) * stride)
    tile = lambda w: pl.BlockSpec((None, n, w), lambda i, j: (i, j, 0))
    layer = p["layer"]
    hspec = pl.BlockSpec((None, pr, 2 * D_FF), lambda i, j: (i, 0, 0))
    hspec_in = pl.BlockSpec((None, None, pr, 2 * D_FF), lambda i, j: (layer, i, 0, 0))
    params = [p["wco"], p["wro"], p["wmo"], p["norm_ffn_g"], p["wup"], p["ffn_w"], p["ffn_b"], p["wdown"]]
    ins = [x, c, y, gate] + ([hist] if has_hist else []) + params + ([final_g] if final_norm else [])
    in_specs = ([tile(D_MODEL), tile(D_CONV), tile(D_MODEL), tile(2 * D_MODEL)] + ([hspec_in] if has_hist else [])
                + [_layer_spec(a, layer) for a in params]
                + ([_const_spec(final_g.shape)] if final_norm else []))
    return pl.pallas_call(
        functools.partial(_back_kernel, n=n, stride=stride, has_hist=has_hist, final_norm=final_norm),
        grid=(nb, t // n),
        in_specs=in_specs,
        out_specs=[tile(D_MODEL), hspec],
        out_shape=[jax.ShapeDtypeStruct((nb, t, D_MODEL), F32), jax.ShapeDtypeStruct((nb, pr, 2 * D_FF), F32)],
        scratch_shapes=[pltpu.VMEM((pr + n, 2 * D_FF), F32)],
        compiler_params=_cparams(2),
        name="back",
    )(*ins)


def _pad_axis(a, axis, size):
    widths = [(0, 0)] * a.ndim
    widths[axis] = (0, size - a.shape[axis])
    return jnp.pad(a, widths)


def _regroup_lora_cols(a):
    wl = a[..., 0:LORA_DECAY]
    al = a[..., LORA_DECAY:LORA_DECAY + LORA_AAA]
    gl = a[..., LORA_DECAY + LORA_AAA:]
    return jnp.concatenate([_pad_axis(wl, -1, LANES_V7X), _pad_axis(al, -1, LANES_V7X),
                            _pad_axis(gl, -1, LORA_PAD - LORA_G_OFF)], axis=-1)


def _ungroup_lora_cols(a):
    return jnp.concatenate([a[..., LORA_W_OFF:LORA_W_OFF + LORA_DECAY], a[..., LORA_A_OFF:LORA_A_OFF + LORA_AAA],
                            a[..., LORA_G_OFF:LORA_G_OFF + LORA_GATE]], axis=-1)


def _stack_weights(norm_mix_g, w_in, conv_dw_w, conv_dw_b, conv_ln_g, conv_ln_b, w_conv_out,
                   rw_mu, rw_w0, rw_w2, rw_a0, rw_a2, rw_g2, rw_k_k, rw_k_a, rw_r_k, rw_ln_g, rw_ln_b, w_rw_out,
                   w_mix_out, norm_ffn_g, w_up, ffn_dw_w, ffn_dw_b, w_down):
    row = lambda a: a.reshape(a.shape[0], 1, -1)
    o = 2 * D_CONV
    mu = row(rw_mu)
    return dict(
        norm_mix_g=row(norm_mix_g),
        wc=w_in[:, :, :o].astype(BF16),
        wrkv=w_in[:, :, o:o + D_RKV].astype(BF16),
        wlora=_regroup_lora_cols(w_in[:, :, o + D_RKV:o + D_RW_IN]).astype(BF16),
        wg=w_in[:, :, o + D_RW_IN:].astype(BF16),
        conv_w=conv_dw_w, conv_b=row(conv_dw_b), conv_lg=row(conv_ln_g), conv_lb=row(conv_ln_b),
        wco=w_conv_out.astype(BF16),
        murkv=mu[:, :, :D_RKV], mulora=_regroup_lora_cols(mu[:, :, D_RKV:]),
        w0=row(rw_w0), w2=_pad_axis(rw_w2, 1, LANES_V7X).astype(BF16),
        a0=row(rw_a0), a2=_pad_axis(rw_a2, 1, LANES_V7X).astype(BF16),
        g2=_pad_axis(rw_g2, 1, LORA_PAD - LORA_G_OFF).astype(BF16),
        kk=row(rw_k_k), ka=row(rw_k_a), rk=row(rw_r_k), lng=row(rw_ln_g), lnb=row(rw_ln_b),
        wro=w_rw_out.astype(BF16), wmo=w_mix_out.astype(BF16),
        norm_ffn_g=row(norm_ffn_g), wup=w_up.astype(BF16),
        ffn_w=ffn_dw_w, ffn_b=row(ffn_dw_b), wdown=w_down.astype(BF16),
    )


def _layer(x, p, final_g, *, stride, chunk, group, conv_n, conv_hist, shift_state, wkv_state, ffn_hist, wkv_layout):
    u, gate, rw, nrkv, nlora = _front(x, shift_state, p, n=min(FRONT_TILE, x.shape[1]), stride=stride)
    c, new_conv = _conv_branch(u, conv_hist, p, n=conv_n, stride=stride)
    to_seq, from_seq = wkv_layout
    y, new_wkv = _wkv(to_seq(rw), wkv_state, p, chunk=chunk, group=group)
    x2, new_ffn = _back(x, c, from_seq(y), gate, ffn_hist, p, final_g, n=min(BACK_TILE, x.shape[1]),
                        stride=stride)
    return x2, new_conv, nrkv, nlora, new_wkv, new_ffn


def kernel(x_prompt, x_sample, state_conv, state_shift, state_wkv, state_ffn,
           norm_mix_g, w_in, conv_dw_w, conv_dw_b, conv_ln_g, conv_ln_b, w_conv_out,
           rw_mu, rw_w0, rw_w2, rw_a0, rw_a2, rw_g2, rw_k_k, rw_k_a, rw_r_k, rw_ln_g, rw_ln_b, w_rw_out,
           w_mix_out, norm_ffn_g, w_up, ffn_dw_w, ffn_dw_b, w_down, norm_final_g):
    params = (norm_mix_g, w_in, conv_dw_w, conv_dw_b, conv_ln_g, conv_ln_b, w_conv_out,
              rw_mu, rw_w0, rw_w2, rw_a0, rw_a2, rw_g2, rw_k_k, rw_k_a, rw_r_k, rw_ln_g, rw_ln_b, w_rw_out,
              w_mix_out, norm_ffn_g, w_up, ffn_dw_w, ffn_dw_b, w_down)
    depth = w_in.shape[0]
    b_p, t_p, _ = x_prompt.shape
    b_s, t_s, _ = x_sample.shape
    final_g = norm_final_g.reshape(1, -1)

    ns = SAMPLE_SPLIT
    stride = b_s // ns

    def tm(a):
        lead, (rows, c) = a.shape[:-3], a.shape[-2:]
        a = jnp.swapaxes(a.reshape(lead + (ns, stride, rows, c)), -3, -2)
        return a.reshape(lead + (ns, rows * stride, c))

    def untm(a, rows):
        lead, c = a.shape[:-3], a.shape[-1]
        a = jnp.swapaxes(a.reshape(lead + (ns, rows, stride, c)), -3, -2)
        return a.reshape(lead + (b_s, rows, c))

    xp = x_prompt
    xs = tm(x_sample)

    ident = lambda a: a
    prompt_layout = (ident, ident)
    sample_to_seq = lambda a: jnp.pad(untm(a, t_s), ((0, 0), (0, SAMPLE_CHUNK - t_s), (0, 0)))
    sample_from_seq = lambda a: tm(a[:, :t_s])

    p_conv, p_shift, p_wkv, p_ffn = [], [], [], []
    s_conv, s_shift, s_ffn = [], [], []
    s_wkv = jnp.zeros(state_wkv.shape, F32)
    stacked = _stack_weights(*params)
    conv_hist = tm(state_conv)
    ffn_hist = tm(state_ffn)
    shift_state = (state_shift[:, :, :D_RKV].reshape(depth, ns, stride, D_RKV),
                   _regroup_lora_cols(state_shift[:, :, D_RKV:]).reshape(depth, ns, stride, LORA_PAD))
    for l in range(depth):
        p = dict(stacked, layer=l)
        fin = final_g if l == depth - 1 else None

        xp, nc, nrkv, nlora, nw, nf = _layer(
            xp, p, fin, stride=1, chunk=PROMPT_CHUNK, group=PROMPT_GROUP, conv_n=CONV_TILE, conv_hist=None,
            shift_state=None, wkv_state=None, ffn_hist=None, wkv_layout=prompt_layout)
        p_conv.append(nc)
        p_shift.append(jnp.concatenate([nrkv[:, -1], _ungroup_lora_cols(nlora[:, -1])], axis=-1))
        p_wkv.append(nw)
        p_ffn.append(nf[:, nf.shape[1] - (FFN_CONV_W - 1):])

        xs, nc, nrkv, nlora, s_wkv, nf = _layer(
            xs, p, fin, stride=stride, chunk=SAMPLE_CHUNK, group=SAMPLE_GROUP, conv_n=t_s * stride,
            conv_hist=conv_hist, shift_state=shift_state, wkv_state=(state_wkv, s_wkv, l), ffn_hist=ffn_hist,
            wkv_layout=(sample_to_seq, sample_from_seq))
        s_conv.append(nc)
        s_shift.append(jnp.concatenate([nrkv.reshape(b_s, D_RKV), _ungroup_lora_cols(nlora.reshape(b_s, LORA_PAD))],
                                       axis=-1))
        s_ffn.append(nf)

    y_prompt = xp
    y_sample = untm(xs, t_s)
    return (y_prompt, y_sample, jnp.stack(p_conv), jnp.stack(p_shift), jnp.stack(p_wkv), jnp.stack(p_ffn),
            untm(jnp.stack(s_conv), CONV_W - 1), jnp.stack(s_shift), s_wkv, untm(jnp.stack(s_ffn), FFN_CONV_W - 1))
```

```python
import functools

import jax
import jax.numpy as jnp
from jax import lax
from jax.experimental import pallas as pl
from jax.experimental.pallas import tpu as pltpu

F32 = jnp.float32
BF16 = jnp.bfloat16

D_MODEL = 1024
D_CONV = D_MODEL // 2
CONV_W = 31
HEAD = 64
N_HEADS = D_MODEL // HEAD
LORA_DECAY = 64
LORA_AAA = 64
LORA_GATE = 160
D_FF = 3 * D_MODEL
FFN_CONV_W = 3
D_RKV = 3 * D_MODEL
RW_PARTS = 6
D_RW_IN = D_RKV + LORA_DECAY + LORA_AAA + LORA_GATE
RMS_EPS = 1e-6
LN_EPS = 1e-5
GN_EPS = 64e-5

SUBLANES_V7X = 8
LANES_V7X = 128
MXU_DIM_V7X = 256
VMEM_LIMIT_BYTES_V7X = 56 * 1024 * 1024

LORA_PAD = 4 * LANES_V7X
LORA_W_OFF, LORA_A_OFF, LORA_G_OFF = 0, LANES_V7X, 2 * LANES_V7X

BACK_TILE = 256
FFN_COL_BLOCK = 256
FRONT_TILE = 256
CONV_TILE = 512
CONV_ROW_BLOCK = 64
PROMPT_CHUNK = 64
SAMPLE_CHUNK = 8
PROMPT_GROUP = 4
SAMPLE_GROUP = 8
SAMPLE_SPLIT = 4


def _cparams(n_grid):
    return pltpu.CompilerParams(dimension_semantics=("arbitrary",) * n_grid,
                                vmem_limit_bytes=VMEM_LIMIT_BYTES_V7X)


def _const_spec(shape):
    return pl.BlockSpec(shape, lambda *_: (0,) * len(shape), pipeline_mode=pl.Buffered(1))


def _layer_spec(a, layer):
    return pl.BlockSpec((None,) + a.shape[1:], lambda *_: (layer,) + (0,) * (a.ndim - 1),
                        pipeline_mode=pl.Buffered(1))


def _dot(a, b):
    return jnp.dot(a, b, preferred_element_type=F32)


def _rmsnorm(x, g):
    return x * lax.rsqrt(jnp.mean(x * x, axis=-1, keepdims=True) + RMS_EPS) * g


def _front_kernel(*refs, n, stride, has_state):
    if has_state:
        (x_ref, srkv_ref, slora_ref, g_ref, wc_ref, wrkv_ref, wlora_ref, wg_ref, murkv_ref, mulora_ref,
         w0_ref, w2_ref, a0_ref, a2_ref, g2_ref,
         u_ref, gate_ref, rw_ref, nrkv_ref, nlora_ref, frkv_scr, flora_scr) = refs
    else:
        (x_ref, g_ref, wc_ref, wrkv_ref, wlora_ref, wg_ref, murkv_ref, mulora_ref,
         w0_ref, w2_ref, a0_ref, a2_ref, g2_ref,
         u_ref, gate_ref, rw_ref, nrkv_ref, nlora_ref, frkv_scr, flora_scr) = refs
    p = max(SUBLANES_V7X, stride)
    rw = lambda i: rw_ref.at[:, i * D_MODEL:(i + 1) * D_MODEL]
    r_ref, k_ref, v_ref, lw_ref, alr_ref, go_ref = (rw(i) for i in range(RW_PARTS))

    @pl.when(pl.program_id(1) == 0)
    def _():
        if has_state:
            frkv_scr[0:p, :] = srkv_ref[...]
            flora_scr[0:p, :] = slora_ref[...]
        else:
            frkv_scr[0:p, :] = jnp.zeros((p, D_RKV), F32)
            flora_scr[0:p, :] = jnp.zeros((p, LORA_PAD), F32)

    hb = _rmsnorm(x_ref[...], g_ref[...]).astype(BF16)
    zc = _dot(hb, wc_ref[...])
    u_ref[...] = zc[:, :D_CONV] * jax.nn.sigmoid(zc[:, D_CONV:])
    gate_ref[...] = jax.nn.sigmoid(_dot(hb, wg_ref[...]))
    zrkv = _dot(hb, wrkv_ref[...])
    zlora = _dot(hb, wlora_ref[...])

    frkv_scr[p:p + n, :] = zrkv
    flora_scr[p:p + n, :] = zlora
    xs = zrkv + (frkv_scr[p - stride:p - stride + n, :] - zrkv) * murkv_ref[...]
    xl = zlora + (flora_scr[p - stride:p - stride + n, :] - zlora) * mulora_ref[...]
    last_rkv = frkv_scr[n:n + p, :]
    last_lora = flora_scr[n:n + p, :]
    nrkv_ref[...] = last_rkv
    nlora_ref[...] = last_lora
    frkv_scr[0:p, :] = last_rkv
    flora_scr[0:p, :] = last_lora

    r_ref[...] = xs[:, 0:D_MODEL]
    k_ref[...] = xs[:, D_MODEL:2 * D_MODEL]
    v_ref[...] = xs[:, 2 * D_MODEL:3 * D_MODEL]

    wl = jnp.tanh(xl[:, LORA_W_OFF:LORA_W_OFF + LANES_V7X]).astype(BF16)
    al = xl[:, LORA_A_OFF:LORA_A_OFF + LANES_V7X].astype(BF16)
    gl = jax.nn.sigmoid(xl[:, LORA_G_OFF:LORA_PAD]).astype(BF16)
    t = -(w0_ref[...] + _dot(wl, w2_ref[...]))
    softplus = jnp.maximum(t, 0.0) + jnp.log(1.0 + jnp.exp(-jnp.abs(t)))
    lw_ref[...] = -jnp.exp(-softplus - 0.5)
    alr_ref[...] = jax.nn.sigmoid(a0_ref[...] + _dot(al, a2_ref[...]))
    go_ref[...] = _dot(gl, g2_ref[...])


def _front(x, shift_state, p, *, n, stride):
    nb, t, _ = x.shape
    has_state = shift_state is not None
    layer = p["layer"]
    pr = max(SUBLANES_V7X, stride)
    tile = lambda c: pl.BlockSpec((None, n, c), lambda i, j: (i, j, 0))
    state = lambda c: pl.BlockSpec((None, pr, c), lambda i, j: (i, 0, 0))
    state_in = lambda c: pl.BlockSpec((None, None, pr, c), lambda i, j: (layer, i, 0, 0))
    params = (p["norm_mix_g"], p["wc"], p["wrkv"], p["wlora"], p["wg"], p["murkv"], p["mulora"],
              p["w0"], p["w2"], p["a0"], p["a2"], p["g2"])
    ins = [x] + (list(shift_state) if has_state else []) + list(params)
    in_specs = ([tile(D_MODEL)] + ([state_in(D_RKV), state_in(LORA_PAD)] if has_state else [])
                + [_layer_spec(a, layer) for a in params])
    full = lambda c: jax.ShapeDtypeStruct((nb, t, c), F32)
    return pl.pallas_call(
        functools.partial(_front_kernel, n=n, stride=stride, has_state=has_state),
        grid=(nb, t // n),
        in_specs=in_specs,
        out_specs=[tile(D_CONV), tile(2 * D_MODEL), tile(RW_PARTS * D_MODEL), state(D_RKV), state(LORA_PAD)],
        out_shape=[full(D_CONV), full(2 * D_MODEL), full(RW_PARTS * D_MODEL),
                   jax.ShapeDtypeStruct((nb, pr, D_RKV), F32), jax.ShapeDtypeStruct((nb, pr, LORA_PAD), F32)],
        scratch_shapes=[pltpu.VMEM((pr + n, D_RKV), F32), pltpu.VMEM((pr + n, LORA_PAD), F32)],
        compiler_params=_cparams(2),
        name="front",
    )(*ins)


def _conv_kernel(*refs, n, n_steps, stride, has_hist):
    refs = list(refs)
    u_ref = refs.pop(0)
    hist_ref = refs.pop(0) if has_hist else None
    w_ref, b_ref, lg_ref, lb_ref, c_ref, newhist_ref, full_scr = refs[:7]
    aligned = stride % SUBLANES_V7X == 0
    shift_scr = None if aligned else refs[7]
    hist_rows = (CONV_W - 1) * stride
    pad = (-hist_rows) % SUBLANES_V7X
    base = pad + hist_rows

    @pl.when(pl.program_id(1) == 0)
    def _():
        if has_hist:
            if pad:
                full_scr[0:pad, :] = jnp.zeros((pad, D_CONV), F32)
            full_scr[pad:base, :] = hist_ref[...]
        else:
            full_scr[0:base, :] = jnp.zeros((base, D_CONV), F32)

    full_scr[base:base + n, :] = u_ref[...]
    if not aligned:
        span = base + n - SUBLANES_V7X
        for s in range(1, SUBLANES_V7X):
            shift_scr[s - 1, 0:span, :] = full_scr[s:s + span, :]

    def tap(j, r0):
        o = pad + j * stride
        s = o % SUBLANES_V7X
        a = r0 + o - s
        if s == 0:
            return full_scr[a:a + CONV_ROW_BLOCK, :]
        return shift_scr[s - 1, a:a + CONV_ROW_BLOCK, :]

    w = w_ref[...]
    for rb in range(n // CONV_ROW_BLOCK):
        r0 = rb * CONV_ROW_BLOCK
        acc = jnp.broadcast_to(b_ref[...], (CONV_ROW_BLOCK, D_CONV))
        for j in range(CONV_W):
            acc = acc + w[j:j + 1, :] * tap(j, r0)
        mu = jnp.mean(acc, axis=-1, keepdims=True)
        xc = acc - mu
        var = jnp.mean(xc * xc, axis=-1, keepdims=True)
        y = xc * lax.rsqrt(var + LN_EPS) * lg_ref[...] + lb_ref[...]
        c_ref[r0:r0 + CONV_ROW_BLOCK, :] = (y * jax.nn.sigmoid(y)).astype(BF16)

    newhist_ref[...] = full_scr[pad + n:base + n, :]
    if n_steps > 1:
        full_scr[0:base, :] = full_scr[n:n + base, :]


def _conv_branch(u, hist, p, *, n, stride):
    nb, t, _ = u.shape
    w, b, lg, lb = p["conv_w"], p["conv_b"], p["conv_lg"], p["conv_lb"]
    has_hist = hist is not None
    hist_rows = (CONV_W - 1) * stride
    base = hist_rows + (-hist_rows) % SUBLANES_V7X
    tile = pl.BlockSpec((None, n, D_CONV), lambda i, j: (i, j, 0))
    layer = p["layer"]
    hspec = pl.BlockSpec((None, hist_rows, D_CONV), lambda i, j: (i, 0, 0))
    hspec_in = pl.BlockSpec((None, None, hist_rows, D_CONV), lambda i, j: (layer, i, 0, 0))
    ins = [u] + ([hist] if has_hist else []) + [w, b, lg, lb]
    in_specs = [tile] + ([hspec_in] if has_hist else []) + [_layer_spec(a, layer) for a in (w, b, lg, lb)]
    scratch = [pltpu.VMEM((base + n, D_CONV), F32)]
    if stride % SUBLANES_V7X:
        scratch.append(pltpu.VMEM((SUBLANES_V7X - 1, base + n, D_CONV), F32))
    return pl.pallas_call(
        functools.partial(_conv_kernel, n=n, n_steps=t // n, stride=stride, has_hist=has_hist),
        grid=(nb, t // n),
        in_specs=in_specs,
        out_specs=[tile, hspec],
        out_shape=[jax.ShapeDtypeStruct((nb, t, D_CONV), BF16), jax.ShapeDtypeStruct((nb, hist_rows, D_CONV), F32)],
        scratch_shapes=scratch,
        compiler_params=_cparams(2),
        name="conv_branch",
    )(*ins)


def _wkv_kernel(*refs, chunk, n_chunks, group, has_state):
    if has_state:
        rw_ref, s0_ref, _, kk_ref, ka_ref, rk_ref, lng_ref, lnb_ref, y_ref, sout_ref, s_scr = refs
    else:
        rw_ref, kk_ref, ka_ref, rk_ref, lng_ref, lnb_ref, y_ref, sout_ref, s_scr = refs
    rw = lambda i: rw_ref.at[:, :, i * D_MODEL:(i + 1) * D_MODEL]
    r_ref, k_ref, v_ref, lw_ref, alr_ref, go_ref = (rw(i) for i in range(RW_PARTS))
    c = chunk
    transposed_state = not has_state

    @pl.when(pl.program_id(1) == 0)
    def _():
        if has_state:
            s_scr[...] = s0_ref[...]
        else:
            s_scr[...] = jnp.zeros_like(s_scr)

    row = lax.broadcasted_iota(jnp.int32, (c, c), 0)
    col = lax.broadcasted_iota(jnp.int32, (c, c), 1)
    incl = row >= col
    strict = row > col
    tri = jnp.where(incl, 1.0, 0.0).astype(BF16)
    sh = lambda x, s: lax.shift_right_logical(x, jnp.int32(s))
    eye = row == col
    first = strict & (sh(row, 1) == sh(col, 1))
    offs = []
    lg = 1
    while (1 << lg) < c:
        offs.append((1 << lg, (sh(row, lg + 1) == sh(col, lg + 1))
                     & ((sh(row, lg) & 1) == 1) & ((sh(col, lg) & 1) == 0)))
        lg += 1

    srow = lax.broadcasted_iota(jnp.int32, (MXU_DIM_V7X, MXU_DIM_V7X), 0)
    scol = lax.broadcasted_iota(jnp.int32, (MXU_DIM_V7X, MXU_DIM_V7X), 1)
    head_lg = HEAD.bit_length() - 1
    blockdiag = jnp.where(sh(srow, head_lg) == sh(scol, head_lg), 1.0, 0.0).astype(BF16)

    def head_sums(xs):
        xb = [x.astype(BF16) for x in xs]
        return [jnp.concatenate([_dot(b[:, q:q + MXU_DIM_V7X], blockdiag) for q in range(0, D_MODEL, MXU_DIM_V7X)],
                                axis=1) for b in xb]

    gs = range(group)
    r = [r_ref[g] for g in gs]
    k = [k_ref[g] for g in gs]
    v = [v_ref[g] for g in gs]
    lw = [lw_ref[g] for g in gs]
    alr = [alr_ref[g] for g in gs]
    kkraw = [k[g] * kk_ref[...] for g in gs]
    kmod = [k[g] * (1.0 + (alr[g] - 1.0) * ka_ref[...]) for g in gs]
    hi = [lw[g].astype(BF16) for g in gs]
    rem = [lw[g] - hi[g].astype(F32) for g in gs]
    mid = [rem[g].astype(BF16) for g in gs]
    lo = [(rem[g] - mid[g].astype(F32)).astype(BF16) for g in gs]
    cum = [_dot(tri, hi[g]) + _dot(tri, mid[g]) + _dot(tri, lo[g]) for g in gs]
    nrm2 = head_sums([kkraw[g] * kkraw[g] for g in gs])
    w_incl = [jnp.exp(cum[g]) for g in gs]
    w_excl = [jnp.exp(cum[g] - lw[g]) for g in gs]
    w_inv = [jnp.exp(-cum[g]) for g in gs]
    w_last = [w_incl[g][c - 1:c, :] for g in gs]
    kk = [kkraw[g] / jnp.maximum(jnp.sqrt(nrm2[g]), 1e-12) for g in gs]
    a_t = [-kk[g] * w_excl[g] for g in gs]
    b_t = [kk[g] * alr[g] * w_inv[g] for g in gs]
    r_t = [r[g] * w_incl[g] for g in gs]
    k_t = [kmod[g] * w_inv[g] for g in gs]
    lhs_f = [jnp.concatenate([a_t[g], r_t[g]], axis=0).astype(BF16) for g in gs]
    rhs_f = [jnp.concatenate([b_t[g], k_t[g]], axis=0) for g in gs]
    v_b = [v[g].astype(BF16) for g in gs]
    lane_groups = range(0, D_MODEL, LANES_V7X)
    if transposed_state:
        rhs_t = [[rhs_f[g][:, q:q + LANES_V7X].T for q in lane_groups] for g in gs]
        decay = [[jnp.broadcast_to(w_last[g][:, q:q + LANES_V7X], (LANES_V7X, LANES_V7X)).T for q in lane_groups]
                 for g in gs]
        rhs_tb = [[x.astype(BF16) for x in rhs_t[g]] for g in gs]
        rhs_wt = [[(x * d[:, :2 * c]).astype(BF16) for x, d in zip(rhs_t[g], decay[g])] for g in gs]
    else:
        rhs_b = [rhs_f[g].astype(BF16) for g in gs]
        rhs_wb = [(rhs_f[g] * w_last[g]).astype(BF16) for g in gs]

    chains = [(g, h) for g in gs for h in range(N_HEADS)]
    sl = lambda h: slice(h * HEAD, (h + 1) * HEAD)
    nt = (((1,), (1,)), ((), ()))
    tn = (((0,), (0,)), ((), ()))
    s_old = [s_scr[g, h] for g, h in chains]
    idx = range(len(chains))
    head_rows = lambda h: slice((h % 2) * HEAD, (h % 2 + 1) * HEAD)
    if transposed_state:
        both = [_dot(lhs_f[g][:, sl(h)],
                     jnp.concatenate([rhs_tb[g][h // 2][head_rows(h), :], s_old[i].astype(BF16)], axis=1))
                for i, (g, h) in enumerate(chains)]
        pm = [both[i][:, :2 * c] for i in idx]
        gm = [both[i][:, 2 * c:] for i in idx]
    else:
        pm = [lax.dot_general(lhs_f[g][:, sl(h)], rhs_b[g][:, sl(h)], nt, preferred_element_type=F32)
              for g, h in chains]
        gm = [lax.dot_general(lhs_f[g][:, sl(h)], s_old[i].astype(BF16), nt, preferred_element_type=F32)
              for i, (g, h) in enumerate(chains)]
    low = [jnp.where(strict, pm[i][:c, :c], 0.0) for i in idx]
    pakv = [_dot(jnp.where(strict, pm[i][:c, c:], 0.0).astype(BF16), v_b[g][:, sl(h)])
            for i, (g, h) in enumerate(chains)]

    def odd_rows(x, s):
        return jnp.concatenate([x[b * s:(b + 1) * s] for b in range(1, c // s, 2)], axis=0)

    def with_odd_rows(x, odd, s):
        return jnp.concatenate([odd[(b // 2) * s:(b // 2 + 1) * s] if b % 2 else x[b * s:(b + 1) * s]
                                for b in range(c // s)], axis=0)

    t = [jnp.where(eye, 1.0, 0.0) + jnp.where(first, low[i], 0.0) for i in idx]
    for s, m in offs:
        tb = [t[i].astype(BF16) for i in idx]
        if s % SUBLANES_V7X:
            a = [_dot(jnp.where(m, low[i], 0.0).astype(BF16), tb[i]) for i in idx]
            t = [t[i] + _dot(tb[i], a[i].astype(BF16)) for i in idx]
        else:
            zero = jnp.zeros((c, c), F32)
            a = [_dot(odd_rows(jnp.where(m, low[i], 0.0), s).astype(BF16), tb[i]) for i in idx]
            t_odd = [odd_rows(t[i], s) for i in idx]
            upd = [_dot(t_odd[i].astype(BF16), with_odd_rows(zero, a[i], s).astype(BF16)) for i in idx]
            t = [with_odd_rows(t[i], t_odd[i] + upd[i], s) for i in idx]
    u = [_dot(t[i].astype(BF16), (gm[i][:c] + pakv[i]).astype(BF16)) for i in idx]
    uv = [jnp.concatenate([u[i].astype(BF16), v_b[g][:, sl(h)]], axis=0) for i, (g, h) in enumerate(chains)]
    m2 = [jnp.concatenate([jnp.where(incl, pm[i][c:, :c], 0.0), jnp.where(incl, pm[i][c:, c:], 0.0)],
                          axis=1).astype(BF16) for i in idx]
    y = [gm[i][c:] + _dot(m2[i], uv[i]) for i in idx]
    for i, (g, h) in enumerate(chains):
        if transposed_state:
            s_scr[g, h] = (s_old[i] * decay[g][h // 2][head_rows(h), :HEAD]
                           + _dot(rhs_wt[g][h // 2][head_rows(h), :], uv[i]))
        else:
            s_scr[g, h] = (s_old[i] * w_last[g][:, sl(h)]
                           + lax.dot_general(uv[i], rhs_wb[g][:, sl(h)], tn, preferred_element_type=F32))

    y_f = [jnp.concatenate(y[g * N_HEADS:(g + 1) * N_HEADS], axis=1) for g in gs]
    mu = head_sums(y_f)
    yc = [y_f[g] - mu[g] * (1.0 / HEAD) for g in gs]
    var_bonus = head_sums([yc[g] * yc[g] for g in gs] + [r[g] * kmod[g] * rk_ref[...] for g in gs])
    for g in gs:
        yn = yc[g] * lax.rsqrt(var_bonus[g] * (1.0 / HEAD) + GN_EPS) * lng_ref[...] + lnb_ref[...]
        y_ref[g] = (yn + var_bonus[group + g] * v[g]) * go_ref[g]

    @pl.when(pl.program_id(1) == n_chunks - 1)
    def _():
        if transposed_state:
            hrow = lax.broadcasted_iota(jnp.int32, (HEAD, HEAD), 0)
            hcol = lax.broadcasted_iota(jnp.int32, (HEAD, HEAD), 1)
            ident = jnp.where(hrow == hcol, 1.0, 0.0).astype(BF16)
            xt = lambda p: lax.dot_general(p, ident, tn, preferred_element_type=F32)
            for g, h in chains:
                st = s_scr[g, h]
                p0 = st.astype(BF16)
                rem0 = st - p0.astype(F32)
                p1 = rem0.astype(BF16)
                p2 = (rem0 - p1.astype(F32)).astype(BF16)
                sout_ref[g, h] = xt(p0) + xt(p1) + xt(p2)
        else:
            sout_ref[...] = s_scr[...]


def _wkv(rw, state, p, *, chunk, group):
    nb, t, _ = rw.shape
    has_state = state is not None
    assert has_state or 2 * chunk == LANES_V7X
    n_chunks = t // chunk
    tile = pl.BlockSpec((group, chunk, D_MODEL), lambda i, j: (i, j, 0))
    tile_in = pl.BlockSpec((group, chunk, RW_PARTS * D_MODEL), lambda i, j: (i, j, 0))
    params = (p["kk"], p["ka"], p["rk"], p["lng"], p["lnb"])
    pspecs = [_layer_spec(a, p["layer"]) for a in params]
    kern = functools.partial(_wkv_kernel, chunk=chunk, n_chunks=n_chunks, group=group, has_state=has_state)
    y_shape = jax.ShapeDtypeStruct((nb, t, D_MODEL), F32)
    scratch = [pltpu.VMEM((group, N_HEADS, HEAD, HEAD), F32)]
    if not has_state:
        sspec = pl.BlockSpec((group, N_HEADS, HEAD, HEAD), lambda i, j: (i, 0, 0, 0))
        return pl.pallas_call(
            kern, grid=(nb // group, n_chunks),
            in_specs=[tile_in] + pspecs,
            out_specs=[tile, sspec],
            out_shape=[y_shape, jax.ShapeDtypeStruct((nb, N_HEADS, HEAD, HEAD), F32)],
            scratch_shapes=scratch, compiler_params=_cparams(2), name="wkv",
        )(rw, *params)
    all_states, out_states, layer = state
    lspec = pl.BlockSpec((None, group, N_HEADS, HEAD, HEAD), lambda i, j: (layer, i, 0, 0, 0))
    in_spec = pl.BlockSpec((group, N_HEADS, HEAD, HEAD), lambda i, j: (i, 0, 0, 0))
    all_states = all_states[layer]
    return pl.pallas_call(
        kern, grid=(nb // group, n_chunks),
        in_specs=[tile_in, in_spec, pl.BlockSpec(memory_space=pl.ANY)] + pspecs,
        out_specs=[tile, lspec],
        out_shape=[y_shape, jax.ShapeDtypeStruct(out_states.shape, F32)],
        input_output_aliases={2: 1},
        scratch_shapes=scratch, compiler_params=_cparams(2), name="wkv",
    )(rw, all_states, out_states, *params)


GELU_C0 = 0.7978845608028654
GELU_C1 = 0.044715


def _back_kernel(*refs, n, stride, has_hist, final_norm):
    refs = list(refs)
    x_ref, c_ref, y_ref, gate_ref = refs[:4]
    del refs[:4]
    hist_ref = refs.pop(0) if has_hist else None
    wco_ref, wro_ref, wmo_ref, gffn_ref, wup_ref, fw_ref, fb_ref, wd_ref = refs[:8]
    del refs[:8]
    gfin_ref = refs.pop(0) if final_norm else None
    o_ref, newhist_ref, full_scr = refs
    hist_rows = (FFN_CONV_W - 1) * stride
    p = max(SUBLANES_V7X, hist_rows)

    @pl.when(pl.program_id(1) == 0)
    def _():
        if has_hist:
            full_scr[0:p, :] = hist_ref[...]
        else:
            full_scr[0:p, :] = jnp.zeros((p, 2 * D_FF), F32)

    ya = _dot(c_ref[...], wco_ref[...])
    yb = _dot(y_ref[...].astype(BF16), wro_ref[...])
    g = gate_ref[...]
    m = g[:, :D_MODEL] * ya + g[:, D_MODEL:] * yb
    x1 = x_ref[...] + _dot(m.astype(BF16), wmo_ref[...])

    h2 = _rmsnorm(x1, gffn_ref[...]).astype(BF16)

    def conv_cols(cols):
        up = _dot(h2, wup_ref[:, cols])
        full_scr[p:p + n, cols] = up
        return (fb_ref[:, cols] + fw_ref[0:1, cols] * full_scr[p - 2 * stride:p - 2 * stride + n, cols]
                + fw_ref[1:2, cols] * full_scr[p - stride:p - stride + n, cols] + fw_ref[2:3, cols] * up)

    pair = lambda q: (conv_cols(slice(q, q + FFN_COL_BLOCK)), conv_cols(slice(D_FF + q, D_FF + q + FFN_COL_BLOCK)))
    x2 = x1
    nxt = pair(0)
    for q in range(0, D_FF, FFN_COL_BLOCK):
        a, gate_lin = nxt
        if q + FFN_COL_BLOCK < D_FF:
            nxt = pair(q + FFN_COL_BLOCK)
        gelu = 0.5 * a * (1.0 + jnp.tanh(GELU_C0 * (a + GELU_C1 * (a * a * a))))
        x2 = x2 + _dot((gelu * gate_lin).astype(BF16), wd_ref[q:q + FFN_COL_BLOCK, :])
    last = full_scr[n:n + p, :]
    newhist_ref[...] = last
    full_scr[0:p, :] = last
    o_ref[...] = _rmsnorm(x2, gfin_ref[...]) if final_norm else x2


def _back(x, c, y, gate, hist, p, final_g, *, n, stride):
    nb, t, _ = x.shape
    has_hist = hist is not None
    final_norm = final_g is not None
    pr = max(SUBLANES_V7X, (FFN_CONV_W - 1) * stride)
    tile = lambda w: pl.BlockSpec((None, n, w), lambda i, j: (i, j, 0))
    layer = p["layer"]
    hspec = pl.BlockSpec((None, pr, 2 * D_FF), lambda i, j: (i, 0, 0))
    hspec_in = pl.BlockSpec((None, None, pr, 2 * D_FF), lambda i, j: (layer, i, 0, 0))
    params = [p["wco"], p["wro"], p["wmo"], p["norm_ffn_g"], p["wup"], p["ffn_w"], p["ffn_b"], p["wdown"]]
    ins = [x, c, y, gate] + ([hist] if has_hist else []) + params + ([final_g] if final_norm else [])
    in_specs = ([tile(D_MODEL), tile(D_CONV), tile(D_MODEL), tile(2 * D_MODEL)] + ([hspec_in] if has_hist else [])
                + [_layer_spec(a, layer) for a in params]
                + ([_const_spec(final_g.shape)] if final_norm else []))
    return pl.pallas_call(
        functools.partial(_back_kernel, n=n, stride=stride, has_hist=has_hist, final_norm=final_norm),
        grid=(nb, t // n),
        in_specs=in_specs,
        out_specs=[tile(D_MODEL), hspec],
        out_shape=[jax.ShapeDtypeStruct((nb, t, D_MODEL), F32), jax.ShapeDtypeStruct((nb, pr, 2 * D_FF), F32)],
        scratch_shapes=[pltpu.VMEM((pr + n, 2 * D_FF), F32)],
        compiler_params=_cparams(2),
        name="back",
    )(*ins)


def _pad_axis(a, axis, size):
    widths = [(0, 0)] * a.ndim
    widths[axis] = (0, size - a.shape[axis])
    return jnp.pad(a, widths)


def _regroup_lora_cols(a):
    wl = a[..., 0:LORA_DECAY]
    al = a[..., LORA_DECAY:LORA_DECAY + LORA_AAA]
    gl = a[..., LORA_DECAY + LORA_AAA:]
    return jnp.concatenate([_pad_axis(wl, -1, LANES_V7X), _pad_axis(al, -1, LANES_V7X),
                            _pad_axis(gl, -1, LORA_PAD - LORA_G_OFF)], axis=-1)


def _ungroup_lora_cols(a):
    return jnp.concatenate([a[..., LORA_W_OFF:LORA_W_OFF + LORA_DECAY], a[..., LORA_A_OFF:LORA_A_OFF + LORA_AAA],
                            a[..., LORA_G_OFF:LORA_G_OFF + LORA_GATE]], axis=-1)


def _stack_weights(norm_mix_g, w_in, conv_dw_w, conv_dw_b, conv_ln_g, conv_ln_b, w_conv_out,
                   rw_mu, rw_w0, rw_w2, rw_a0, rw_a2, rw_g2, rw_k_k, rw_k_a, rw_r_k, rw_ln_g, rw_ln_b, w_rw_out,
                   w_mix_out, norm_ffn_g, w_up, ffn_dw_w, ffn_dw_b, w_down):
    row = lambda a: a.reshape(a.shape[0], 1, -1)
    o = 2 * D_CONV
    mu = row(rw_mu)
    return dict(
        norm_mix_g=row(norm_mix_g),
        wc=w_in[:, :, :o].astype(BF16),
        wrkv=w_in[:, :, o:o + D_RKV].astype(BF16),
        wlora=_regroup_lora_cols(w_in[:, :, o + D_RKV:o + D_RW_IN]).astype(BF16),
        wg=w_in[:, :, o + D_RW_IN:].astype(BF16),
        conv_w=conv_dw_w, conv_b=row(conv_dw_b), conv_lg=row(conv_ln_g), conv_lb=row(conv_ln_b),
        wco=w_conv_out.astype(BF16),
        murkv=mu[:, :, :D_RKV], mulora=_regroup_lora_cols(mu[:, :, D_RKV:]),
        w0=row(rw_w0), w2=_pad_axis(rw_w2, 1, LANES_V7X).astype(BF16),
        a0=row(rw_a0), a2=_pad_axis(rw_a2, 1, LANES_V7X).astype(BF16),
        g2=_pad_axis(rw_g2, 1, LORA_PAD - LORA_G_OFF).astype(BF16),
        kk=row(rw_k_k), ka=row(rw_k_a), rk=row(rw_r_k), lng=row(rw_ln_g), lnb=row(rw_ln_b),
        wro=w_rw_out.astype(BF16), wmo=w_mix_out.astype(BF16),
        norm_ffn_g=row(norm_ffn_g), wup=w_up.astype(BF16),
        ffn_w=ffn_dw_w, ffn_b=row(ffn_dw_b), wdown=w_down.astype(BF16),
    )


def _layer(x, p, final_g, *, stride, chunk, group, conv_n, conv_hist, shift_state, wkv_state, ffn_hist, wkv_layout):
    u, gate, rw, nrkv, nlora = _front(x, shift_state, p, n=min(FRONT_TILE, x.shape[1]), stride=stride)
    c, new_conv = _conv_branch(u, conv_hist, p, n=conv_n, stride=stride)
    to_seq, from_seq = wkv_layout
    y, new_wkv = _wkv(to_seq(rw), wkv_state, p, chunk=chunk, group=group)
    x2, new_ffn = _back(x, c, from_seq(y), gate, ffn_hist, p, final_g, n=min(BACK_TILE, x.shape[1]),
                        stride=stride)
    return x2, new_conv, nrkv, nlora, new_wkv, new_ffn


def kernel(x_prompt, x_sample, state_conv, state_shift, state_wkv, state_ffn,
           norm_mix_g, w_in, conv_dw_w, conv_dw_b, conv_ln_g, conv_ln_b, w_conv_out,
           rw_mu, rw_w0, rw_w2, rw_a0, rw_a2, rw_g2, rw_k_k, rw_k_a, rw_r_k, rw_ln_g, rw_ln_b, w_rw_out,
           w_mix_out, norm_ffn_g, w_up, ffn_dw_w, ffn_dw_b, w_down, norm_final_g):
    params = (norm_mix_g, w_in, conv_dw_w, conv_dw_b, conv_ln_g, conv_ln_b, w_conv_out,
              rw_mu, rw_w0, rw_w2, rw_a0, rw_a2, rw_g2, rw_k_k, rw_k_a, rw_r_k, rw_ln_g, rw_ln_b, w_rw_out,
              w_mix_out, norm_ffn_g, w_up, ffn_dw_w, ffn_dw_b, w_down)
    depth = w_in.shape[0]
    b_p, t_p, _ = x_prompt.shape
    b_s, t_s, _ = x_sample.shape
    final_g = norm_final_g.reshape(1, -1)

    ns = SAMPLE_SPLIT
    stride = b_s // ns

    def tm(a):
        lead, (rows, c) = a.shape[:-3], a.shape[-2:]
        a = jnp.swapaxes(a.reshape(lead + (ns, stride, rows, c)), -3, -2)
        return a.reshape(lead + (ns, rows * stride, c))

    def untm(a, rows):
        lead, c = a.shape[:-3], a.shape[-1]
        a = jnp.swapaxes(a.reshape(lead + (ns, rows, stride, c)), -3, -2)
        return a.reshape(lead + (b_s, rows, c))

    xp = x_prompt
    xs = tm(x_sample)

    ident = lambda a: a
    prompt_layout = (ident, ident)
    sample_to_seq = lambda a: jnp.pad(untm(a, t_s), ((0, 0), (0, SAMPLE_CHUNK - t_s), (0, 0)))
    sample_from_seq = lambda a: tm(a[:, :t_s])

    p_conv, p_shift, p_wkv, p_ffn = [], [], [], []
    s_conv, s_shift, s_ffn = [], [], []
    s_wkv = jnp.zeros(state_wkv.shape, F32)
    stacked = _stack_weights(*params)
    conv_hist = tm(state_conv)
    ffn_hist = tm(state_ffn)
    shift_state = (state_shift[:, :, :D_RKV].reshape(depth, ns, stride, D_RKV),
                   _regroup_lora_cols(state_shift[:, :, D_RKV:]).reshape(depth, ns, stride, LORA_PAD))
    for l in range(depth):
        p = dict(stacked, layer=l)
        fin = final_g if l == depth - 1 else None

        xp, nc, nrkv, nlora, nw, nf = _layer(
            xp, p, fin, stride=1, chunk=PROMPT_CHUNK, group=PROMPT_GROUP, conv_n=CONV_TILE, conv_hist=None,
            shift_state=None, wkv_state=None, ffn_hist=None, wkv_layout=prompt_layout)
        p_conv.append(nc)
        p_shift.append(jnp.concatenate([nrkv[:, -1], _ungroup_lora_cols(nlora[:, -1])], axis=-1))
        p_wkv.append(nw)
        p_ffn.append(nf[:, nf.shape[1] - (FFN_CONV_W - 1):])

        xs, nc, nrkv, nlora, s_wkv, nf = _layer(
            xs, p, fin, stride=stride, chunk=SAMPLE_CHUNK, group=SAMPLE_GROUP, conv_n=t_s * stride,
            conv_hist=conv_hist, shift_state=shift_state, wkv_state=(state_wkv, s_wkv, l), ffn_hist=ffn_hist,
            wkv_layout=(sample_to_seq, sample_from_seq))
        s_conv.append(nc)
        s_shift.append(jnp.concatenate([nrkv.reshape(b_s, D_RKV), _ungroup_lora_cols(nlora.reshape(b_s, LORA_PAD))],
                                       axis=-1))
        s_ffn.append(nf)

    y_prompt = xp
    y_sample = untm(xs, t_s)
    return (y_prompt, y_sample, jnp.stack(p_conv), jnp.stack(p_shift), jnp.stack(p_wkv), jnp.stack(p_ffn),
            untm(jnp.stack(s_conv), CONV_W - 1), jnp.stack(s_shift), s_wkv, untm(jnp.stack(s_ffn), FFN_CONV_W - 1))
```
